```python
import math
import jax, jax.numpy as jnp
from jax import lax
import numpy as np

D_MODEL = 1024
BATCH = 4
SEQ = 8192
DEPTH = 2

CTX_LEN = 256
GRID_W = 64
ROPE_THETA = 10000.0
EPS = 1e-6
Q_BLOCK = 128

MLA_HEADS = 4
MLA_Q_RANK = 256
MLA_KV_RANK = 128
MLA_NOPE = 64
MLA_ROPE = 32
MLA_V = 64
MLA_QK = MLA_NOPE + MLA_ROPE
DIFF_HEADS = 4
DIFF_DK = 32
DIFF_DV = 2 * DIFF_DK
GQA_HEADS = 8
GQA_KV_HEADS = 2
GQA_DH = 64

MIX_WIDTH = MLA_HEADS * MLA_V + DIFF_HEADS * DIFF_DV + GQA_HEADS * GQA_DH
COL_SIZES = (MLA_Q_RANK, MLA_KV_RANK, MLA_ROPE,
             DIFF_HEADS * 2 * DIFF_DK, DIFF_HEADS * 2 * DIFF_DK, DIFF_HEADS * DIFF_DV,
             GQA_HEADS * GQA_DH, GQA_KV_HEADS * GQA_DH, GQA_KV_HEADS * GQA_DH)
IN_COLS = sum(COL_SIZES)

N_EXPERTS = 16
EXPERT_FF = 2048
CAPACITY_FACTOR = 2

kernel_name = "hybrid_parallel_heads_mla_diff_gqa_ec_moe_dit"


def rms_norm(x, g):
    xf = x.astype(jnp.float32)
    y = xf * lax.rsqrt(jnp.mean(xf * xf, axis=-1, keepdims=True) + EPS)
    return (y * g.astype(jnp.float32)).astype(x.dtype)


def modulate(h, shift, scale):
    return h * (1 + scale[:, None]) + shift[:, None]


def axial_rope_tables(rows, rot_dim):
    r = jnp.broadcast_to(jnp.arange(rows, dtype=jnp.float32)[:, None], (rows, GRID_W)).reshape(-1)
    col = jnp.broadcast_to(jnp.arange(GRID_W, dtype=jnp.float32)[None, :], (rows, GRID_W)).reshape(-1)
    axis_dim = rot_dim // 2
    inv = ROPE_THETA ** (-jnp.arange(0, axis_dim, 2, dtype=jnp.float32) / axis_dim)
    ang = jnp.concatenate([r[:, None] * inv, col[:, None] * inv], axis=-1)
    return jnp.cos(ang), jnp.sin(ang)


def apply_rope(x, tab):
    cos, sin = tab
    shp = (1, cos.shape[0]) + (1,) * (x.ndim - 3) + (cos.shape[1],)
    cos = cos.reshape(shp)
    sin = sin.reshape(shp)
    xf = x.astype(jnp.float32).reshape(x.shape[:-1] + (-1, 2))
    x0, x1 = xf[..., 0], xf[..., 1]
    out = jnp.stack([x0 * cos - x1 * sin, x0 * sin + x1 * cos], axis=-1)
    return out.reshape(x.shape).astype(x.dtype)


def blocked_attention(q, k, v, map_w):
    B, S, M, H, dk = q.shape
    G = k.shape[3]
    R = H // G
    dv = v.shape[-1]
    nb = S // Q_BLOCK
    qb = q.reshape(B, nb, Q_BLOCK, M, G, R, dk).transpose(1, 0, 2, 3, 4, 5, 6)
    scale = dk ** -0.5
    w = map_w.astype(jnp.float32)

    def block(qi):
        s = jnp.einsum("bqmgrd,btmgd->bmgrqt", qi, k).astype(jnp.float32) * scale
        p = jnp.einsum("m,bmgrqt->bgrqt", w, jax.nn.softmax(s, axis=-1))
        return jnp.einsum("bgrqt,btgd->bqgrd", p.astype(v.dtype), v)

    o = lax.map(block, qb)
    return o.transpose(1, 0, 2, 3, 4, 5).reshape(B, S, H, dv)


def project_qkv(h, w_in, g_cq, g_ckv, w_uq, w_ukv, g_mla_q, g_mla_k,
                g_diff_q, g_diff_k, g_gqa_q, g_gqa_k, rope):
    B, n, _ = h.shape
    z = h @ w_in
    offs = [int(o) for o in np.cumsum(COL_SIZES)[:-1]]
    z_cq, z_ckv, z_kr, z_dq, z_dk, z_dv, z_gq, z_gk, z_gv = jnp.split(z, offs, axis=-1)
    q_m = (rms_norm(z_cq, g_cq) @ w_uq).reshape(B, n, MLA_HEADS, MLA_QK)
    kv_m = (rms_norm(z_ckv, g_ckv) @ w_ukv).reshape(B, n, MLA_HEADS, MLA_NOPE + MLA_V)
    k_rope = jnp.broadcast_to(z_kr[:, :, None, :], (B, n, MLA_HEADS, MLA_ROPE))
    k_m = jnp.concatenate([kv_m[..., :MLA_NOPE], k_rope], axis=-1)
    v_m = kv_m[..., MLA_NOPE:]
    q_m = rms_norm(q_m, g_mla_q)
    k_m = rms_norm(k_m, g_mla_k)
    q_d = rms_norm(z_dq.reshape(B, n, DIFF_HEADS, 2, DIFF_DK).transpose(0, 1, 3, 2, 4), g_diff_q)
    k_d = rms_norm(z_dk.reshape(B, n, DIFF_HEADS, 2, DIFF_DK).transpose(0, 1, 3, 2, 4), g_diff_k)
    v_d = z_dv.reshape(B, n, DIFF_HEADS, DIFF_DV)
    q_g = rms_norm(z_gq.reshape(B, n, GQA_HEADS, GQA_DH), g_gqa_q)[:, :, None]
    k_g = rms_norm(z_gk.reshape(B, n, GQA_KV_HEADS, GQA_DH), g_gqa_k)[:, :, None]
    v_g = z_gv.reshape(B, n, GQA_KV_HEADS, GQA_DH)
    if rope is not None:
        tab_m, tab_d, tab_g = rope
        q_m = jnp.concatenate([q_m[..., :MLA_NOPE], apply_rope(q_m[..., MLA_NOPE:], tab_m)], axis=-1)
        k_m = jnp.concatenate([k_m[..., :MLA_NOPE], apply_rope(k_m[..., MLA_NOPE:], tab_m)], axis=-1)
        q_d = apply_rope(q_d, tab_d)
        k_d = apply_rope(k_d, tab_d)
        q_g = apply_rope(q_g, tab_g)
        k_g = apply_rope(k_g, tab_g)
    return ((q_m[:, :, None], k_m[:, :, None], v_m), (q_d, k_d, v_d), (q_g, k_g, v_g))


def mixer_outputs(q_src, kv_srcs, lam, lam_init, g_diff_out):
    def keys(i):
        return jnp.concatenate([s[i][1] for s in kv_srcs], axis=1)

    def vals(i):
        return jnp.concatenate([s[i][2] for s in kv_srcs], axis=1)

    B, n = q_src[0][0].shape[:2]
    one = jnp.ones((1,), jnp.float32)
    o_mla = blocked_attention(q_src[0][0], keys(0), vals(0), one)
    w_diff = jnp.stack([jnp.ones((), jnp.float32), -lam])
    o_diff = blocked_attention(q_src[1][0], keys(1), vals(1), w_diff)
    o_diff = rms_norm(o_diff, g_diff_out) * (1.0 - lam_init)
    o_gqa = blocked_attention(q_src[2][0], keys(2), vals(2), one)
    return jnp.concatenate([o_mla.reshape(B, n, -1), o_diff.reshape(B, n, -1),
                            o_gqa.reshape(B, n, -1)], axis=-1)


def expert_choice_ffn(h, w_router, w_gate, w_up, w_down):
    B, n, D = h.shape
    cap = CAPACITY_FACTOR * n // N_EXPERTS
    aff = jax.nn.softmax((h @ w_router).astype(jnp.float32), axis=-1)
    g, idx = lax.top_k(aff.transpose(0, 2, 1), cap)
    xe = jax.vmap(lambda hb, ib: hb[ib])(h, idx)
    hid = jax.nn.silu(jnp.einsum("becd,edf->becf", xe, w_gate)) * jnp.einsum("becd,edf->becf", xe, w_up)
    ye = jnp.einsum("becf,efd->becd", hid, w_down) * g[..., None].astype(h.dtype)
    return jax.vmap(lambda hb, ib, yb: jnp.zeros_like(hb).at[ib.reshape(-1)].add(yb.reshape(-1, D)))(h, idx, ye)


def setup_inputs(seed: int = 0) -> dict:
    key = jax.random.key(seed)
    ks = iter(jax.random.split(key, 40))

    def nrm(shape, scale):
        return jax.random.normal(next(ks), shape, jnp.float32) * scale

    def gain(shape):
        return 1.0 + nrm(shape, 0.05)

    L, D = DEPTH, D_MODEL
    return {
        "x": nrm((BATCH, SEQ, D), 1.0),
        "c": nrm((BATCH, D), 1.0),
        "ctx": nrm((BATCH, CTX_LEN, D), 1.0),
        "c_ctx": nrm((D,), 1.0),
        "w_ada": nrm((L, D, 6 * D), 0.5 * D ** -0.5),
        "b_ada": nrm((L, 6 * D), 0.02),
        "g_norm1": gain((L, D)),
        "w_in": nrm((L, D, IN_COLS), D ** -0.5),
        "g_cq": gain((L, MLA_Q_RANK)),
        "g_ckv": gain((L, MLA_KV_RANK)),
        "w_uq": nrm((L, MLA_Q_RANK, MLA_HEADS * MLA_QK), MLA_Q_RANK ** -0.5),
        "w_ukv": nrm((L, MLA_KV_RANK, MLA_HEADS * (MLA_NOPE + MLA_V)), MLA_KV_RANK ** -0.5),
        "g_mla_q": gain((L, MLA_QK)),
        "g_mla_k": gain((L, MLA_QK)),
        "g_diff_q": gain((L, DIFF_DK)),
        "g_diff_k": gain((L, DIFF_DK)),
        "lam_q1": nrm((L, DIFF_DK), 0.1),
        "lam_k1": nrm((L, DIFF_DK), 0.1),
        "lam_q2": nrm((L, DIFF_DK), 0.1),
        "lam_k2": nrm((L, DIFF_DK), 0.1),
        "g_diff_out": gain((L, DIFF_DV)),
        "g_gqa_q": gain((L, GQA_DH)),
        "g_gqa_k": gain((L, GQA_DH)),
        "w_o": nrm((L, MIX_WIDTH, D), MIX_WIDTH ** -0.5),
        "g_norm2": gain((L, D)),
        "w_router": nrm((L, D, N_EXPERTS), D ** -0.5),
        "w_gate": nrm((L, N_EXPERTS, D, EXPERT_FF), D ** -0.5),
        "w_up": nrm((L, N_EXPERTS, D, EXPERT_FF), D ** -0.5),
        "w_down": nrm((L, N_EXPERTS, EXPERT_FF, D), EXPERT_FF ** -0.5),
    }


def reference(x, c, ctx, c_ctx, w_ada, b_ada, g_norm1, w_in, g_cq, g_ckv, w_uq, w_ukv,
              g_mla_q, g_mla_k, g_diff_q, g_diff_k, lam_q1, lam_k1, lam_q2, lam_k2,
              g_diff_out, g_gqa_q, g_gqa_k, w_o, g_norm2, w_router, w_gate, w_up, w_down):
    ROWS = x.shape[1] // GRID_W
    rope = (axial_rope_tables(ROWS, MLA_ROPE), axial_rope_tables(ROWS, DIFF_DK),
            axial_rope_tables(ROWS, GQA_DH))
    c_ctx_row = c_ctx[None, :]
    for l in range(DEPTH):
        last = l == DEPTH - 1
        sh1, sc1, ga1, sh2, sc2, ga2 = jnp.split(jax.nn.silu(c) @ w_ada[l] + b_ada[l], 6, axis=-1)
        csh1, csc1, cga1, csh2, csc2, cga2 = jnp.split(
            jax.nn.silu(c_ctx_row) @ w_ada[l] + b_ada[l], 6, axis=-1)

        def proj(h, tabs):
            return project_qkv(h, w_in[l], g_cq[l], g_ckv[l], w_uq[l], w_ukv[l], g_mla_q[l], g_mla_k[l],
                               g_diff_q[l], g_diff_k[l], g_gqa_q[l], g_gqa_k[l], tabs)

        def moe(h):
            return expert_choice_ffn(h, w_router[l], w_gate[l], w_up[l], w_down[l])

        lam_init = 0.8 - 0.6 * math.exp(-0.3 * l)
        lam = (jnp.exp(jnp.sum(lam_q1[l].astype(jnp.float32) * lam_k1[l].astype(jnp.float32)))
               - jnp.exp(jnp.sum(lam_q2[l].astype(jnp.float32) * lam_k2[l].astype(jnp.float32)))
               + lam_init)

        qkv_x = proj(modulate(rms_norm(x, g_norm1[l]), sh1, sc1), rope)
        qkv_c = proj(modulate(rms_norm(ctx, g_norm1[l]), csh1, csc1), None)

        att_x = mixer_outputs(qkv_x, [qkv_x, qkv_c], lam, lam_init, g_diff_out[l])
        x = x + ga1[:, None] * (att_x @ w_o[l])
        x = x + ga2[:, None] * moe(modulate(rms_norm(x, g_norm2[l]), sh2, sc2))

        if not last:
            att_c = mixer_outputs(qkv_c, [qkv_c], lam, lam_init, g_diff_out[l])
            ctx = ctx + cga1[:, None] * (att_c @ w_o[l])
            ctx = ctx + cga2[:, None] * moe(modulate(rms_norm(ctx, g_norm2[l]), csh2, csc2))
    return x
```

```python
import functools
import math

import jax
import jax.numpy as jnp
from jax import lax
from jax.experimental import pallas as pl
from jax.experimental.pallas import tpu as pltpu

F32 = jnp.float32
BF16 = jnp.bfloat16
I32 = jnp.int32

GRID_W = 64
ROPE_THETA = 10000.0
EPS = 1e-6

MLA_HEADS, MLA_Q_RANK, MLA_KV_RANK, MLA_NOPE, MLA_ROPE, MLA_V = 4, 256, 128, 64, 32, 64
MLA_QK = MLA_NOPE + MLA_ROPE
DIFF_HEADS, DIFF_DK = 4, 32
DIFF_DV = 2 * DIFF_DK
GQA_HEADS, GQA_KV_HEADS, GQA_DH = 8, 2, 64
HEAD_V = 64
N_EXPERTS = 16
CAPACITY_FACTOR = 2

O_CQ, O_CKV, O_KR, O_DQ, O_DK, O_DV, O_GQ, O_GK, O_GV, IN_COLS = (
    0, 256, 384, 416, 672, 928, 1184, 1696, 1824, 1952)
G_CQ, G_CKV, G_MQ, G_MK, G_DQ, G_DKK, G_GQ, G_GK, G_ROWS = 0, 256, 384, 480, 576, 608, 640, 704, 768

N_MAPS = 20
N_KSLAB = 7
N_VHEAD = 10
LANES = 128
LOG2E = 1.4426950408889634
NEG_BIG = -1e30

V7X_VMEM_LIMIT = 56 * 1024 * 1024


def _map_kslab(mp):
    if mp < 4:
        return mp
    if mp < 12:
        return 4 + (mp - 4) // 4
    return 6


def _map_vhead(mp):
    if mp < 4:
        return mp
    if mp < 12:
        return 4 + (mp - 4) // 2
    return 8 + (mp - 12) // 4


def _cparams(sem, vmem=None):
    return pltpu.CompilerParams(dimension_semantics=sem, vmem_limit_bytes=vmem)


def _ada_kernel(c_ref, w_ref, b_ref, o_ref):
    a = c_ref[...]
    s = a / (1.0 + jnp.exp(-a))
    o_ref[0] = jnp.dot(s.astype(BF16), w_ref[0].astype(BF16),
                       preferred_element_type=F32) + b_ref[0]


def _ada_call(cc, w_ada, b_ada):
    L, D, D6 = w_ada.shape
    R = cc.shape[0]
    tn = 1536
    return pl.pallas_call(
        _ada_kernel,
        grid=(L, D6 // tn),
        in_specs=[pl.BlockSpec((R, D), lambda l, j: (0, 0)),
                  pl.BlockSpec((1, D, tn), lambda l, j: (l, 0, j)),
                  pl.BlockSpec((1, 1, tn), lambda l, j: (l, 0, j))],
        out_specs=pl.BlockSpec((1, R, tn), lambda l, j: (l, 0, j)),
        out_shape=jax.ShapeDtypeStruct((L, R, D6), F32),
        compiler_params=_cparams(("parallel", "parallel"), 40 * 1024 * 1024),
        name="adaln",
    )(cc, w_ada, b_ada.reshape(L, 1, D6))


def _rms_rows(x3, g):
    ms = jnp.mean(x3 * x3, axis=1, keepdims=True)
    return x3 * lax.rsqrt(ms + EPS) * g[None]


def _rope_rows(x3, cos, sin_s):
    G, d, T = x3.shape
    x2 = x3.reshape(G * d, T)
    up = pltpu.roll(x2, G * d - 1, 0)
    dn = pltpu.roll(x2, 1, 0)
    even = (lax.broadcasted_iota(I32, (G * d, T), 0) % 2) == 0
    sw = jnp.where(even, up, dn).reshape(G, d, T)
    return x3 * cos[None] + sw * sin_s[None]


def _proj_kernel(x_ref, mod_ref, g1_ref, gcol_ref, rope_ref, win_ref, wuq_ref, wukv_ref,
                 q_ref, k_ref, v_ref):
    TM = x_ref.shape[1]
    x = x_ref[0]
    ms = jnp.mean(x * x, axis=-1, keepdims=True)
    h = (x * lax.rsqrt(ms + EPS) * g1_ref[0]) * (1.0 + mod_ref[0, 0, 1:2, :]) + mod_ref[0, 0, 0:1, :]
    hb = h.astype(BF16)
    zt = lax.dot_general(win_ref[0], hb, (((1,), (1,)), ((), ())),
                         preferred_element_type=F32)

    gcol = gcol_ref[0]
    cos32, sin32 = rope_ref[0:32, :], rope_ref[32:64, :]
    cos64, sin64 = rope_ref[64:128, :], rope_ref[128:192, :]
    zero32 = jnp.zeros((MLA_HEADS, 32, TM), F32)

    cqn = _rms_rows(zt[O_CQ:O_CKV][None], gcol[G_CQ:G_CKV])[0]
    qm = jnp.dot(wuq_ref[0], cqn.astype(BF16), preferred_element_type=F32)
    qm3 = _rms_rows(qm.reshape(MLA_HEADS, MLA_QK, TM), gcol[G_MQ:G_MK])
    qm_r = _rope_rows(qm3[:, MLA_NOPE:, :], cos32, sin32)
    sc = MLA_QK ** -0.5 * LOG2E
    q_ref[0, 0:4] = jnp.concatenate([qm3[:, :MLA_NOPE, :] * sc, qm_r * sc, zero32], axis=1).astype(BF16)

    ckvn = _rms_rows(zt[O_CKV:O_KR][None], gcol[G_CKV:G_MQ])[0]
    kv = jnp.dot(wukv_ref[0], ckvn.astype(BF16), preferred_element_type=F32)
    kv3 = kv.reshape(MLA_HEADS, MLA_NOPE + MLA_V, TM)
    nope = kv3[:, :MLA_NOPE, :]
    kr = zt[O_KR:O_DQ]
    ssq = jnp.sum(nope * nope, axis=1, keepdims=True) + jnp.sum(kr * kr, axis=0, keepdims=True)[None]
    inv = lax.rsqrt(ssq * (1.0 / MLA_QK) + EPS)
    gmk = gcol[G_MK:G_DQ]
    nope_n = nope * inv * gmk[None, :MLA_NOPE]
    kr_n = _rope_rows(kr[None] * inv * gmk[None, MLA_NOPE:], cos32, sin32)
    km = jnp.concatenate([nope_n, kr_n, zero32], axis=1)
    for hh in range(MLA_HEADS):
        k_ref[0, hh] = km[hh].T.astype(BF16)
    v_ref[0, 0:4] = kv3[:, MLA_NOPE:, :].astype(BF16)

    qd = _rope_rows(_rms_rows(zt[O_DQ:O_DK].reshape(8, DIFF_DK, TM), gcol[G_DQ:G_DKK]), cos32, sin32)
    qd = qd * (DIFF_DK ** -0.5 * LOG2E)
    z32 = jnp.zeros((32, TM), F32)
    for gi in range(8):
        j = gi % 4
        parts = [z32] * j + [qd[gi]] + [z32] * (3 - j)
        q_ref[0, 4 + gi] = jnp.concatenate(parts, axis=0).astype(BF16)
    kd = _rope_rows(_rms_rows(zt[O_DK:O_DV].reshape(8, DIFF_DK, TM), gcol[G_DKK:G_GQ]), cos32, sin32)
    kd2 = kd.reshape(2, LANES, TM)
    for a in range(2):
        k_ref[0, 4 + a] = kd2[a].T.astype(BF16)
    v_ref[0, 4:8] = zt[O_DV:O_GQ].reshape(DIFF_HEADS, DIFF_DV, TM).astype(BF16)

    qg = _rope_rows(_rms_rows(zt[O_GQ:O_GK].reshape(GQA_HEADS, GQA_DH, TM), gcol[G_GQ:G_GK]), cos64, sin64)
    qg = qg * (GQA_DH ** -0.5 * LOG2E)
    z64 = jnp.zeros((64, TM), F32)
    for hh in range(GQA_HEADS):
        parts = [qg[hh], z64] if hh < 4 else [z64, qg[hh]]
        q_ref[0, 12 + hh] = jnp.concatenate(parts, axis=0).astype(BF16)
    kg = _rope_rows(_rms_rows(zt[O_GK:O_GV].reshape(GQA_KV_HEADS, GQA_DH, TM), gcol[G_GK:G_ROWS]), cos64, sin64)
    k_ref[0, 6] = kg.reshape(LANES, TM).T.astype(BF16)
    v_ref[0, 8:10] = zt[O_GV:IN_COLS].reshape(GQA_KV_HEADS, GQA_DH, TM).astype(BF16)


def _proj_call(xc, mod, g1, gcol, rope_t, w_in_t, w_uq_t, w_ukv_t, *, tm, n_lat_tiles):
    B, N, D = xc.shape
    nt = N // tm
    stream = lambda i: jnp.minimum(i // n_lat_tiles, 1)
    return pl.pallas_call(
        _proj_kernel,
        grid=(B, nt),
        in_specs=[
            pl.BlockSpec((1, tm, D), lambda b, i: (b, i, 0)),
            pl.BlockSpec((1, 1, 2, D), lambda b, i: (b, stream(i), 0, 0)),
            pl.BlockSpec((1, 1, D), lambda b, i: (0, 0, 0)),
            pl.BlockSpec((1, G_ROWS, 1), lambda b, i: (0, 0, 0)),
            pl.BlockSpec((192, tm), lambda b, i: (0, i)),
            pl.BlockSpec((1, IN_COLS, D), lambda b, i: (0, 0, 0)),
            pl.BlockSpec((1, MLA_HEADS * MLA_QK, MLA_Q_RANK), lambda b, i: (0, 0, 0)),
            pl.BlockSpec((1, MLA_HEADS * (MLA_NOPE + MLA_V), MLA_KV_RANK), lambda b, i: (0, 0, 0)),
        ],
        out_specs=[
            pl.BlockSpec((1, N_MAPS, LANES, tm), lambda b, i: (b, 0, 0, i)),
            pl.BlockSpec((1, N_KSLAB, tm, LANES), lambda b, i: (b, 0, i, 0)),
            pl.BlockSpec((1, N_VHEAD, HEAD_V, tm), lambda b, i: (b, 0, 0, i)),
        ],
        out_shape=[
            jax.ShapeDtypeStruct((B, N_MAPS, LANES, N), BF16),
            jax.ShapeDtypeStruct((B, N_KSLAB, N, LANES), BF16),
            jax.ShapeDtypeStruct((B, N_VHEAD, HEAD_V, N), BF16),
        ],
        compiler_params=_cparams(("parallel", "parallel"), V7X_VMEM_LIMIT),
        name="qkv_proj",
    )(xc, mod, g1, gcol, rope_t, w_in_t, w_uq_t, w_ukv_t)


def _attn_kernel(lam_ref, gout_ref, q_ref, k_ref, v_ref, *rest, lam_init, aliased):
    if aliased:
        _, o_ref, m_sc, l_sc, acc_sc = rest
    else:
        o_ref, m_sc, l_sc, acc_sc = rest
    ki = pl.program_id(2)
    TQ = q_ref.shape[3]

    @pl.when(ki == 0)
    def _():
        m_sc[...] = jnp.full(m_sc.shape, NEG_BIG, F32)
        l_sc[...] = jnp.zeros(l_sc.shape, F32)
        acc_sc[...] = jnp.zeros(acc_sc.shape, F32)

    for mp in range(N_MAPS):
        s = jnp.dot(k_ref[0, _map_kslab(mp)], q_ref[0, mp], preferred_element_type=F32)
        m_old = m_sc[mp]
        m_new = jnp.maximum(m_old, jnp.max(s, axis=0, keepdims=True))
        alpha = jnp.exp2(m_old - m_new)
        p = jnp.exp2(s - m_new)
        l_sc[mp] = alpha * l_sc[mp] + jnp.sum(p, axis=0, keepdims=True)
        acc_sc[mp] = alpha * acc_sc[mp] + jnp.dot(v_ref[0, _map_vhead(mp)], p.astype(BF16),
                                                 preferred_element_type=F32)
        m_sc[mp] = m_new

    @pl.when(ki == pl.num_programs(2) - 1)
    def _():
        lv = lam_ref[...]
        lam = (jnp.exp(jnp.sum(lv[0:1] * lv[1:2], axis=1, keepdims=True))
               - jnp.exp(jnp.sum(lv[2:3] * lv[3:4], axis=1, keepdims=True)) + lam_init)
        heads = []
        for hh in range(MLA_HEADS):
            heads.append(acc_sc[hh] / l_sc[hh])
        for hh in range(DIFF_HEADS):
            a, b = 4 + 2 * hh, 5 + 2 * hh
            od = acc_sc[a] / l_sc[a] - lam * (acc_sc[b] / l_sc[b])
            ms = jnp.mean(od * od, axis=0, keepdims=True)
            heads.append(od * lax.rsqrt(ms + EPS) * gout_ref[...] * (1.0 - lam_init))
        for hh in range(GQA_HEADS):
            heads.append(acc_sc[12 + hh] / l_sc[12 + hh])
        for j in range(len(heads) // 2):
            pair = jnp.concatenate([heads[2 * j], heads[2 * j + 1]], axis=0)
            o_ref[0, :, j * LANES:(j + 1) * LANES] = pair.T.astype(BF16)


def _attn_call(lamv, gout, q, k, v, att_in, *, lam_init, tq, tk, q_off, n_q, k_off, n_k, n_tok):
    B = q.shape[0]
    qo, ko = q_off // tq, k_off // tk
    in_specs = [
        pl.BlockSpec((4, DIFF_DK), lambda b, i, j: (0, 0)),
        pl.BlockSpec((HEAD_V, 1), lambda b, i, j: (0, 0)),
        pl.BlockSpec((1, N_MAPS, LANES, tq), lambda b, i, j: (b, 0, 0, qo + i)),
        pl.BlockSpec((1, N_KSLAB, tk, LANES), lambda b, i, j: (b, 0, ko + j, 0)),
        pl.BlockSpec((1, N_VHEAD, HEAD_V, tk), lambda b, i, j: (b, 0, 0, ko + j)),
    ]
    args = [lamv, gout, q, k, v]
    aliases = {}
    if att_in is not None:
        in_specs.append(pl.BlockSpec(memory_space=pl.ANY))
        args.append(att_in)
        aliases = {5: 0}
    return pl.pallas_call(
        functools.partial(_attn_kernel, lam_init=lam_init, aliased=att_in is not None),
        grid=(B, n_q // tq, n_k // tk),
        in_specs=in_specs,
        out_specs=pl.BlockSpec((1, tq, 16 * HEAD_V), lambda b, i, j: (b, qo + i, 0)),
        out_shape=jax.ShapeDtypeStruct((B, n_tok, 16 * HEAD_V), BF16),
        scratch_shapes=[pltpu.VMEM((N_MAPS, 1, tq), F32), pltpu.VMEM((N_MAPS, 1, tq), F32),
                        pltpu.VMEM((N_MAPS, HEAD_V, tq), F32)],
        input_output_aliases=aliases,
        compiler_params=_cparams(("parallel", "parallel", "arbitrary"), V7X_VMEM_LIMIT),
        name="attention",
    )(*args)


def _post_kernel(att_ref, x_ref, mod_ref, g2_ref, wo_ref, wr_ref, x1_ref, h2_ref, aff_ref):
    a = jnp.dot(att_ref[0], wo_ref[0], preferred_element_type=F32)
    x1 = x_ref[0] + mod_ref[0, 0, 0:1, :] * a
    x1_ref[0] = x1
    ms = jnp.mean(x1 * x1, axis=-1, keepdims=True)
    h2 = (x1 * lax.rsqrt(ms + EPS) * g2_ref[0]) * (1.0 + mod_ref[0, 0, 2:3, :]) + mod_ref[0, 0, 1:2, :]
    hb = h2.astype(BF16)
    h2_ref[0] = hb
    lg = lax.dot_general(wr_ref[0], hb, (((1,), (1,)), ((), ())), preferred_element_type=F32)
    ex = jnp.exp(lg - jnp.max(lg, axis=0, keepdims=True))
    aff_ref[0] = ex / jnp.sum(ex, axis=0, keepdims=True)


def _post_call(att, xc, mod, g2, w_o, w_r_t, *, tm, n_lat_tiles, n_out):
    B, _, D = xc.shape
    nt = n_out // tm
    E = w_r_t.shape[1]
    stream = lambda i: jnp.minimum(i // n_lat_tiles, 1)
    return pl.pallas_call(
        _post_kernel,
        grid=(B, nt),
        in_specs=[
            pl.BlockSpec((1, tm, D), lambda b, i: (b, i, 0)),
            pl.BlockSpec((1, tm, D), lambda b, i: (b, i, 0)),
            pl.BlockSpec((1, 1, 3, D), lambda b, i: (b, stream(i), 0, 0)),
            pl.BlockSpec((1, 1, D), lambda b, i: (0, 0, 0)),
            pl.BlockSpec((1, D, D), lambda b, i: (0, 0, 0)),
            pl.BlockSpec((1, E, D), lambda b, i: (0, 0, 0)),
        ],
        out_specs=[
            pl.BlockSpec((1, tm, D), lambda b, i: (b, i, 0)),
            pl.BlockSpec((1, tm, D), lambda b, i: (b, i, 0)),
            pl.BlockSpec((1, E, tm), lambda b, i: (b, 0, i)),
        ],
        out_shape=[
            jax.ShapeDtypeStruct((B, n_out, D), F32),
            jax.ShapeDtypeStruct((B, n_out, D), BF16),
            jax.ShapeDtypeStruct((B, E, n_out), F32),
        ],
        compiler_params=_cparams(("parallel", "parallel"), 40 * 1024 * 1024),
        name="outproj_router",
    )(att, xc, mod, g2, w_o, w_r_t)


def _route_kernel(aff_ref, pos_ref, post_ref, p_ref, *, cap, tt):
    E, n = aff_ref.shape[1], aff_ref.shape[2]
    bits = lax.bitcast_convert_type(aff_ref[0], I32)

    def count_ge(th):
        return jnp.sum(jnp.where(bits >= th, 1.0, 0.0), axis=1, keepdims=True)

    def search(_, lohi):
        lo, hi = lohi
        mid = lo + lax.shift_right_logical(hi - lo + 1, 1)
        ok = count_ge(mid) >= cap
        return jnp.where(ok, mid, lo), jnp.where(ok, hi, mid - 1)

    lo0 = jnp.zeros((E, 1), I32)
    hi0 = jnp.full((E, 1), 0x7F800000, I32)
    tau, _ = lax.fori_loop(0, 32, search, (lo0, hi0))
    n_gt = jnp.sum(jnp.where(bits > tau, 1.0, 0.0), axis=1, keepdims=True)
    ties_kept = cap - n_gt

    ri = lax.broadcasted_iota(I32, (LANES, LANES), 0)
    ci = lax.broadcasted_iota(I32, (LANES, LANES), 1)
    strict_upper = jnp.where(ri < ci, 1.0, 0.0).astype(BF16)
    lane = lax.broadcasted_iota(I32, (E, LANES), 1)
    filler = jnp.full((LANES - E, LANES), -1.0, F32)

    c_eq = jnp.zeros((E, 1), F32)
    c_sel = jnp.zeros((E, 1), F32)
    starts = jnp.zeros((E, LANES), F32)
    for j in range(n // LANES):
        if (j * LANES) % tt == 0:
            starts = jnp.where(lane == (j * LANES) // tt, c_sel, starts)
        blk = bits[:, j * LANES:(j + 1) * LANES]
        gt = blk > tau
        eq = blk == tau
        eq_f = jnp.where(eq, 1.0, 0.0)
        rank = jnp.dot(eq_f.astype(BF16), strict_upper, preferred_element_type=F32) + c_eq
        sel = gt | (eq & (rank < ties_kept))
        sel_f = jnp.where(sel, 1.0, 0.0)
        pos = jnp.dot(sel_f.astype(BF16), strict_upper, preferred_element_type=F32) + c_sel
        pos = jnp.where(sel, pos, -1.0)
        pos_ref[0, :, j * LANES:(j + 1) * LANES] = pos.astype(I32)
        post_ref[0, j * LANES:(j + 1) * LANES, :] = (
            jnp.concatenate([pos, filler], axis=0).T[:, 0:E].astype(I32))
        c_eq = c_eq + jnp.sum(eq_f, axis=1, keepdims=True)
        c_sel = c_sel + jnp.sum(sel_f, axis=1, keepdims=True)
    p_ref[0] = starts.astype(I32)


def _route_call(aff_t, *, n, col_block, cap, tt):
    B, E, _ = aff_t.shape
    return pl.pallas_call(
        functools.partial(_route_kernel, cap=cap, tt=tt),
        grid=(B,),
        in_specs=[pl.BlockSpec((1, E, n), lambda b: (b, 0, col_block))],
        out_specs=[pl.BlockSpec((1, E, n), lambda b: (b, 0, 0)),
                   pl.BlockSpec((1, n, E), lambda b: (b, 0, 0)),
                   pl.BlockSpec((1, E, LANES), lambda b: (b, 0, 0))],
        out_shape=[jax.ShapeDtypeStruct((B, E, n), I32),
                   jax.ShapeDtypeStruct((B, n, E), I32),
                   jax.ShapeDtypeStruct((B, E, LANES), I32)],
        compiler_params=_cparams(("parallel",), 40 * 1024 * 1024),
        name="route",
    )(aff_t)


def _dispatch_kernel(p_ref, pos_ref, aff_ref, h_ref, xe_ref, gs_ref, acc, gacc, *, cap, tt, sb, n_exp):
    b, e = pl.program_id(0), pl.program_id(1)
    n = pos_ref.shape[2]
    win = 2 * sb
    acc[...] = jnp.zeros(acc.shape, F32)
    gacc[...] = jnp.zeros(gacc.shape, F32)
    base = (b * n_exp + e) * LANES

    def tile(i, carry):
        row0 = pl.multiple_of((p_ref[base + i] // sb) * sb, sb)
        t0 = pl.multiple_of(i * tt, tt)
        rel = pos_ref[0, :, pl.ds(t0, tt)] - row0
        hit = lax.broadcasted_iota(I32, (win, tt), 0) == rel
        rows = jnp.dot(jnp.where(hit, 1.0, 0.0).astype(BF16), h_ref[0, pl.ds(t0, tt), :],
                       preferred_element_type=F32)
        acc[pl.ds(row0, win), :] += rows
        gacc[pl.ds(row0, win), :] += jnp.sum(jnp.where(hit, aff_ref[0, :, pl.ds(t0, tt)], 0.0),
                                             axis=1, keepdims=True)
        return carry

    lax.fori_loop(0, n // tt, tile, 0)
    xe_ref[0, 0] = acc[0:cap, :].astype(BF16)
    gs_ref[0, 0] = gacc[0:cap, :]


def _dispatch_call(starts, pos, aff, h2, *, n, row_block, cap, tt, sb):
    B, E, _ = pos.shape
    D = h2.shape[2]
    return pl.pallas_call(
        functools.partial(_dispatch_kernel, cap=cap, tt=tt, sb=sb, n_exp=E),
        grid_spec=pltpu.PrefetchScalarGridSpec(
            num_scalar_prefetch=1,
            grid=(B, E),
            in_specs=[
                pl.BlockSpec((1, 1, n), lambda b, e, p: (b * E + e, 0, 0)),
                pl.BlockSpec((1, 1, n), lambda b, e, p: (b * E + e, 0, 0)),
                pl.BlockSpec((1, n, D), lambda b, e, p: (b, row_block, 0), pipeline_mode=pl.Buffered(1)),
            ],
            out_specs=[pl.BlockSpec((1, 1, cap, D), lambda b, e, p: (e, b, 0, 0)),
                       pl.BlockSpec((1, 1, cap, 1), lambda b, e, p: (e, b, 0, 0))],
            scratch_shapes=[pltpu.VMEM((cap + sb, D), F32), pltpu.VMEM((cap + sb, 1), F32)],
        ),
        out_shape=[jax.ShapeDtypeStruct((E, B, cap, D), BF16),
                   jax.ShapeDtypeStruct((E, B, cap, 1), F32)],
        compiler_params=_cparams(("parallel", "arbitrary"), V7X_VMEM_LIMIT),
        name="moe_dispatch",
    )(starts.reshape(-1), pos.reshape(B * E, 1, n), aff.reshape(B * E, 1, n), h2)


def _ffn_kernel(x_ref, gs_ref, wg_ref, wu_ref, wd_ref, y_ref, *, fc):
    nb, cap, D = x_ref.shape[1], x_ref.shape[2], x_ref.shape[3]
    FF = wg_ref.shape[2]
    x = x_ref[0].reshape(nb * cap, D)
    y = jnp.zeros((nb * cap, D), F32)
    for c in range(FF // fc):
        g = jnp.dot(x, wg_ref[0, :, c * fc:(c + 1) * fc], preferred_element_type=F32)
        u = jnp.dot(x, wu_ref[0, :, c * fc:(c + 1) * fc], preferred_element_type=F32)
        hid = (g / (1.0 + jnp.exp(-g))) * u
        y = y + jnp.dot(hid.astype(BF16), wd_ref[0, c * fc:(c + 1) * fc, :], preferred_element_type=F32)
    y = y * gs_ref[0].reshape(nb * cap, 1)
    y_ref[0] = y.reshape(nb, cap, D).astype(BF16)


def _ffn_call(xe, gs, wg, wu, wd, *, nb):
    E, B, cap, D = xe.shape
    FF = wg.shape[2]
    return pl.pallas_call(
        functools.partial(_ffn_kernel, fc=512),
        grid=(E, B // nb),
        in_specs=[
            pl.BlockSpec((1, nb, cap, D), lambda e, b: (e, b, 0, 0)),
            pl.BlockSpec((1, nb, cap, 1), lambda e, b: (e, b, 0, 0)),
            pl.BlockSpec((1, D, FF), lambda e, b: (e, 0, 0)),
            pl.BlockSpec((1, D, FF), lambda e, b: (e, 0, 0)),
            pl.BlockSpec((1, FF, D), lambda e, b: (e, 0, 0)),
        ],
        out_specs=pl.BlockSpec((1, nb, cap, D), lambda e, b: (e, b, 0, 0)),
        out_shape=jax.ShapeDtypeStruct((E, B, cap, D), BF16),
        compiler_params=_cparams(("parallel", "arbitrary"), V7X_VMEM_LIMIT),
        name="moe_ffn",
    )(xe, gs, wg, wu, wd)


def _combine_kernel(p_ref, x1_ref, post_ref, ga_ref, *rest, sb, eg, n_exp, aliased):
    ye_refs = rest[:2 * eg]
    if aliased:
        _, o_ref, acc = rest[2 * eg:]
    else:
        o_ref, acc = rest[2 * eg:]
    b, i, g = pl.program_id(0), pl.program_id(1), pl.program_id(2)
    tt = x1_ref.shape[1]

    @pl.when(g == 0)
    def _():
        acc[...] = jnp.zeros(acc.shape, F32)

    post = post_ref[0]
    lane = lax.broadcasted_iota(I32, post.shape, 1)
    slot = lax.broadcasted_iota(I32, (tt, sb), 1)
    for r in range(eg):
        e = g * eg + r
        col = jnp.sum(jnp.where(lane == e, post, 0), axis=1, keepdims=True)
        j0 = p_ref[(b * n_exp + e) * LANES + i] // sb
        for k in range(2):
            hit = col == (slot + (j0 + k) * sb)
            acc[...] += jnp.dot(jnp.where(hit, 1.0, 0.0).astype(BF16), ye_refs[2 * r + k][0, 0],
                                preferred_element_type=F32)

    @pl.when(g == pl.num_programs(2) - 1)
    def _():
        o_ref[0] = x1_ref[0] + ga_ref[0, 0] * acc[...]


def _combine_call(starts, x1, post, ga, ye, x_in, *, n, row_block, cap, tt, sb, eg, n_tok):
    E, B, _, D = ye.shape
    nblk = cap // sb
    rb = row_block * (n // tt)

    def ye_spec(r, k):
        def imap(b, i, g, p):
            e = g * eg + r
            return (e, b, jnp.minimum(p[(b * E + e) * LANES + i] // sb + k, nblk - 1), 0)
        return pl.BlockSpec((1, 1, sb, D), imap)

    in_specs = [
        pl.BlockSpec((1, tt, D), lambda b, i, g, p: (b, rb + i, 0)),
        pl.BlockSpec((1, tt, E), lambda b, i, g, p: (b, i, 0)),
        pl.BlockSpec((1, 1, 1, D), lambda b, i, g, p: (b, min(row_block, 1), 0, 0)),
    ] + [ye_spec(r, k) for r in range(eg) for k in range(2)]
    args = [x1, post, ga] + [ye] * (2 * eg)
    aliases = {}
    if x_in is not None:
        in_specs.append(pl.BlockSpec(memory_space=pl.ANY))
        args.append(x_in)
        aliases = {len(args): 0}
    return pl.pallas_call(
        functools.partial(_combine_kernel, sb=sb, eg=eg, n_exp=E, aliased=x_in is not None),
        grid_spec=pltpu.PrefetchScalarGridSpec(
            num_scalar_prefetch=1,
            grid=(B, n // tt, E // eg),
            in_specs=in_specs,
            out_specs=pl.BlockSpec((1, tt, D), lambda b, i, g, p: (b, rb + i, 0)),
            scratch_shapes=[pltpu.VMEM((tt, D), F32)],
        ),
        out_shape=jax.ShapeDtypeStruct((B, n_tok, D), F32),
        input_output_aliases=aliases,
        compiler_params=_cparams(("parallel", "parallel", "arbitrary"), 40 * 1024 * 1024),
        name="moe_combine",
    )(starts.reshape(-1), *args)


def _rope_tables_t(S, n_ctx):
    rows = S // GRID_W
    r = jnp.broadcast_to(jnp.arange(rows, dtype=F32)[:, None], (rows, GRID_W)).reshape(-1)
    col = jnp.broadcast_to(jnp.arange(GRID_W, dtype=F32)[None, :], (rows, GRID_W)).reshape(-1)
    out = []
    for rot in (MLA_ROPE, GQA_DH):
        axis_dim = rot // 2
        inv = ROPE_THETA ** (-jnp.arange(0, axis_dim, 2, dtype=F32) / axis_dim)
        ang = jnp.concatenate([r[:, None] * inv, col[:, None] * inv], axis=-1)
        cos = jnp.repeat(jnp.cos(ang), 2, axis=1)
        sin = jnp.repeat(jnp.sin(ang), 2, axis=1) * jnp.tile(jnp.array([-1.0, 1.0], F32), rot // 2)
        cos = jnp.concatenate([cos, jnp.ones((n_ctx, rot), F32)], axis=0)
        sin = jnp.concatenate([sin, jnp.zeros((n_ctx, rot), F32)], axis=0)
        out += [cos.T, sin.T]
    return jnp.concatenate(out, axis=0)


def _pick(n, candidates):
    for c in candidates:
        if n % c == 0:
            return c
    raise ValueError(f"no tile size for {n}")


def _moe_stream(aff_t, h2, x1, ga, wg, wu, wd, x_in, *, n, row_block, n_tok, nb):
    B, E, _ = aff_t.shape
    cap = CAPACITY_FACTOR * n // E
    tt = _pick(n, (256, 128))
    sb = min(tt, cap)
    pos, post, starts = _route_call(aff_t, n=n, col_block=row_block, cap=cap, tt=tt)
    aff_s = lax.slice_in_dim(aff_t, row_block * n, (row_block + 1) * n, axis=2)
    xe, gs = _dispatch_call(starts, pos, aff_s, h2, n=n, row_block=row_block, cap=cap, tt=tt, sb=sb)
    ye = _ffn_call(xe, gs, wg, wu, wd, nb=nb)
    return _combine_call(starts, x1, post, ga, ye, x_in, n=n, row_block=row_block, cap=cap, tt=tt, sb=sb,
                         eg=4, n_tok=n_tok)


def kernel(x, c, ctx, c_ctx, w_ada, b_ada, g_norm1, w_in, g_cq, g_ckv, w_uq, w_ukv, g_mla_q, g_mla_k,
           g_diff_q, g_diff_k, lam_q1, lam_k1, lam_q2, lam_k2, g_diff_out, g_gqa_q, g_gqa_k, w_o, g_norm2,
           w_router, w_gate, w_up, w_down):
    B, S, D = x.shape
    n_ctx = ctx.shape[1]
    N = S + n_ctx
    L = w_ada.shape[0]
    assert S % n_ctx == 0 and S % GRID_W == 0
    tm = _pick(n_ctx, (256, 128))
    n_lat_tiles = S // tm
    tq = _pick(S, (512, 256, 128))
    tk = _pick(N, (768, 512, 384, 256, 128))

    R = -(-(B + 1) // 8) * 8
    cc = jnp.zeros((R, D), F32).at[:B].set(c).at[B].set(c_ctx)
    mods = _ada_call(cc, w_ada, b_ada).reshape(L, R, 6, D)

    rope_t = _rope_tables_t(S, n_ctx)
    w_in_t = jnp.swapaxes(w_in, 1, 2).astype(BF16)
    w_uq_t = jnp.swapaxes(w_uq, 1, 2).astype(BF16)
    w_ukv_t = jnp.swapaxes(w_ukv, 1, 2).astype(BF16)
    w_o_b = w_o.astype(BF16)
    w_r_t = jnp.swapaxes(w_router, 1, 2).astype(BF16)
    wg_b, wu_b, wd_b = w_gate.astype(BF16), w_up.astype(BF16), w_down.astype(BF16)
    gcol = jnp.concatenate([g_cq, g_ckv, g_mla_q, g_mla_k, g_diff_q, g_diff_k, g_gqa_q, g_gqa_k],
                           axis=1)[:, :, None]
    lamv = jnp.stack([lam_q1, lam_k1, lam_q2, lam_k2], axis=1)

    xc = jnp.concatenate([x, ctx], axis=1)
    for l in range(L):
        last = l == L - 1
        lam_init = 0.8 - 0.6 * math.exp(-0.3 * l)
        m = mods[l]

        def per_stream(idx):
            lat = m[:B][:, idx, :]
            cx = jnp.broadcast_to(m[B][idx, :][None], lat.shape)
            return jnp.stack([lat, cx], axis=1)

        q, k, v = _proj_call(xc, per_stream(jnp.array([0, 1])), g_norm1[l][None, None], gcol[l][None],
                             rope_t, w_in_t[l][None], w_uq_t[l][None], w_ukv_t[l][None],
                             tm=tm, n_lat_tiles=n_lat_tiles)
        gout = g_diff_out[l][:, None]
        att = _attn_call(lamv[l], gout, q, k, v, None, lam_init=lam_init, tq=tq, tk=tk,
                         q_off=0, n_q=S, k_off=0, n_k=N, n_tok=N)
        n_out = S if last else N
        if not last:
            att = _attn_call(lamv[l], gout, q, k, v, att, lam_init=lam_init, tq=n_ctx, tk=n_ctx,
                             q_off=S, n_q=n_ctx, k_off=S, n_k=n_ctx, n_tok=N)
        x1, h2, aff_t = _post_call(att, xc, per_stream(jnp.array([2, 3, 4])), g_norm2[l][None, None],
                                   w_o_b[l][None], w_r_t[l][None], tm=tm, n_lat_tiles=n_lat_tiles, n_out=n_out)
        ga2 = per_stream(jnp.array([5]))
        xc = _moe_stream(aff_t, h2, x1, ga2, wg_b[l], wu_b[l], wd_b[l], None,
                         n=S, row_block=0, n_tok=n_out, nb=1)
        if not last:
            xc = _moe_stream(aff_t, h2, x1, ga2, wg_b[l], wu_b[l], wd_b[l], xc,
                             n=n_ctx, row_block=S // n_ctx, n_tok=n_out, nb=B)
    return xc
```

```python
import functools
import math

import jax
import jax.numpy as jnp
from jax import lax
from jax.experimental import pallas as pl
from jax.experimental.pallas import tpu as pltpu

F32 = jnp.float32
BF16 = jnp.bfloat16
I32 = jnp.int32

GRID_W = 64
ROPE_THETA = 10000.0
EPS = 1e-6

MLA_HEADS, MLA_Q_RANK, MLA_KV_RANK, MLA_NOPE, MLA_ROPE, MLA_V = 4, 256, 128, 64, 32, 64
MLA_QK = MLA_NOPE + MLA_ROPE
DIFF_HEADS, DIFF_DK = 4, 32
DIFF_DV = 2 * DIFF_DK
GQA_HEADS, GQA_KV_HEADS, GQA_DH = 8, 2, 64
HEAD_V = 64
HEAD_VP = HEAD_V + 8
QK_LOOKAHEAD = 2
N_EXPERTS = 16
CAPACITY_FACTOR = 2

O_CQ, O_CKV, O_KR, O_DQ, O_DK, O_DV, O_GQ, O_GK, O_GV, IN_COLS = (
    0, 256, 384, 416, 672, 928, 1184, 1696, 1824, 1952)
G_CQ, G_CKV, G_MQ, G_MK, G_DQ, G_DKK, G_GQ, G_GK, G_ROWS = 0, 256, 384, 480, 576, 608, 640, 704, 768

N_MAPS = 20
N_KSLAB = 7
N_VHEAD = 10
LANES = 128
LOG2E = 1.4426950408889634
NEG_BIG = -1e30

V7X_VMEM_LIMIT = 56 * 1024 * 1024


def _map_kslab(mp):
    if mp < 4:
        return mp
    if mp < 12:
        return 4 + (mp - 4) // 4
    return 6


def _map_vhead(mp):
    if mp < 4:
        return mp
    if mp < 12:
        return 4 + (mp - 4) // 2
    return 8 + (mp - 12) // 4


def _cparams(sem, vmem=None):
    return pltpu.CompilerParams(dimension_semantics=sem, vmem_limit_bytes=vmem)


def _ada_kernel(c_ref, w_ref, b_ref, o_ref):
    a = c_ref[...]
    s = a / (1.0 + jnp.exp(-a))
    o_ref[0] = jnp.dot(s.astype(BF16), w_ref[0].astype(BF16),
                       preferred_element_type=F32) + b_ref[0]


def _ada_call(cc, w_ada, b_ada):
    L, D, D6 = w_ada.shape
    R = cc.shape[0]
    tn = 1536
    return pl.pallas_call(
        _ada_kernel,
        grid=(L, D6 // tn),
        in_specs=[pl.BlockSpec((R, D), lambda l, j: (0, 0)),
                  pl.BlockSpec((1, D, tn), lambda l, j: (l, 0, j)),
                  pl.BlockSpec((1, 1, tn), lambda l, j: (l, 0, j))],
        out_specs=pl.BlockSpec((1, R, tn), lambda l, j: (l, 0, j)),
        out_shape=jax.ShapeDtypeStruct((L, R, D6), F32),
        compiler_params=_cparams(("parallel", "parallel"), 40 * 1024 * 1024),
        name="adaln",
    )(cc, w_ada, b_ada.reshape(L, 1, D6))


def _rms_rows(x3, g):
    ms = jnp.mean(x3 * x3, axis=1, keepdims=True)
    return x3 * lax.rsqrt(ms + EPS) * g[None]


def _rope_rows(x3, cos, sin_s):
    G, d, T = x3.shape
    x2 = x3.reshape(G * d, T)
    up = pltpu.roll(x2, G * d - 1, 0)
    dn = pltpu.roll(x2, 1, 0)
    even = (lax.broadcasted_iota(I32, (G * d, T), 0) % 2) == 0
    sw = jnp.where(even, up, dn).reshape(G, d, T)
    return x3 * cos[None] + sw * sin_s[None]


def _proj_kernel(x_ref, mod_ref, g1_ref, gcol_ref, rope_ref, win_ref, wuq_ref, wukv_ref,
                 q_ref, k_ref, v_ref):
    TM = x_ref.shape[1]
    x = x_ref[0]
    ms = jnp.mean(x * x, axis=-1, keepdims=True)
    h = (x * lax.rsqrt(ms + EPS) * g1_ref[0]) * (1.0 + mod_ref[0, 0, 1:2, :]) + mod_ref[0, 0, 0:1, :]
    hb = h.astype(BF16)
    zt = lax.dot_general(win_ref[0], hb, (((1,), (1,)), ((), ())),
                         preferred_element_type=F32)

    gcol = gcol_ref[0]
    cos32, sin32 = rope_ref[0:32, :], rope_ref[32:64, :]
    cos64, sin64 = rope_ref[64:128, :], rope_ref[128:192, :]
    zero32 = jnp.zeros((MLA_HEADS, 32, TM), F32)

    cqn = _rms_rows(zt[O_CQ:O_CKV][None], gcol[G_CQ:G_CKV])[0]
    qm = jnp.dot(wuq_ref[0], cqn.astype(BF16), preferred_element_type=F32)
    qm3 = _rms_rows(qm.reshape(MLA_HEADS, MLA_QK, TM), gcol[G_MQ:G_MK])
    qm_r = _rope_rows(qm3[:, MLA_NOPE:, :], cos32, sin32)
    sc = MLA_QK ** -0.5 * LOG2E
    q_ref[0, 0:4] = jnp.concatenate([qm3[:, :MLA_NOPE, :] * sc, qm_r * sc, zero32], axis=1).astype(BF16)

    ckvn = _rms_rows(zt[O_CKV:O_KR][None], gcol[G_CKV:G_MQ])[0]
    kv = jnp.dot(wukv_ref[0], ckvn.astype(BF16), preferred_element_type=F32)
    kv3 = kv.reshape(MLA_HEADS, MLA_NOPE + MLA_V, TM)
    nope = kv3[:, :MLA_NOPE, :]
    kr = zt[O_KR:O_DQ]
    ssq = jnp.sum(nope * nope, axis=1, keepdims=True) + jnp.sum(kr * kr, axis=0, keepdims=True)[None]
    inv = lax.rsqrt(ssq * (1.0 / MLA_QK) + EPS)
    gmk = gcol[G_MK:G_DQ]
    nope_n = nope * inv * gmk[None, :MLA_NOPE]
    kr_n = _rope_rows(kr[None] * inv * gmk[None, MLA_NOPE:], cos32, sin32)
    km = jnp.concatenate([nope_n, kr_n, zero32], axis=1)
    for hh in range(MLA_HEADS):
        k_ref[0, hh] = km[hh].T.astype(BF16)
    def with_ones(v3):
        tail = jnp.where(lax.broadcasted_iota(I32, (v3.shape[0], HEAD_VP - HEAD_V, TM), 1) == 0, 1.0, 0.0)
        return jnp.concatenate([v3, tail], axis=1).astype(BF16)

    v_ref[0, 0:4] = with_ones(kv3[:, MLA_NOPE:, :])

    qd = _rope_rows(_rms_rows(zt[O_DQ:O_DK].reshape(8, DIFF_DK, TM), gcol[G_DQ:G_DKK]), cos32, sin32)
    qd = qd * (DIFF_DK ** -0.5 * LOG2E)
    z32 = jnp.zeros((32, TM), F32)
    for gi in range(8):
        j = gi % 4
        parts = [z32] * j + [qd[gi]] + [z32] * (3 - j)
        q_ref[0, 4 + gi] = jnp.concatenate(parts, axis=0).astype(BF16)
    kd = _rope_rows(_rms_rows(zt[O_DK:O_DV].reshape(8, DIFF_DK, TM), gcol[G_DKK:G_GQ]), cos32, sin32)
    kd2 = kd.reshape(2, LANES, TM)
    for a in range(2):
        k_ref[0, 4 + a] = kd2[a].T.astype(BF16)
    v_ref[0, 4:8] = with_ones(zt[O_DV:O_GQ].reshape(DIFF_HEADS, DIFF_DV, TM))

    qg = _rope_rows(_rms_rows(zt[O_GQ:O_GK].reshape(GQA_HEADS, GQA_DH, TM), gcol[G_GQ:G_GK]), cos64, sin64)
    qg = qg * (GQA_DH ** -0.5 * LOG2E)
    z64 = jnp.zeros((64, TM), F32)
    for hh in range(GQA_HEADS):
        parts = [qg[hh], z64] if hh < 4 else [z64, qg[hh]]
        q_ref[0, 12 + hh] = jnp.concatenate(parts, axis=0).astype(BF16)
    kg = _rope_rows(_rms_rows(zt[O_GK:O_GV].reshape(GQA_KV_HEADS, GQA_DH, TM), gcol[G_GK:G_ROWS]), cos64, sin64)
    k_ref[0, 6] = kg.reshape(LANES, TM).T.astype(BF16)
    v_ref[0, 8:10] = with_ones(zt[O_GV:IN_COLS].reshape(GQA_KV_HEADS, GQA_DH, TM))


def _proj_call(xc, mod, g1, gcol, rope_t, w_in_t, w_uq_t, w_ukv_t, *, tm, n_lat_tiles):
    B, N, D = xc.shape
    nt = N // tm
    stream = lambda i: jnp.minimum(i // n_lat_tiles, 1)
    return pl.pallas_call(
        _proj_kernel,
        grid=(B, nt),
        in_specs=[
            pl.BlockSpec((1, tm, D), lambda b, i: (b, i, 0)),
            pl.BlockSpec((1, 1, 2, D), lambda b, i: (b, stream(i), 0, 0)),
            pl.BlockSpec((1, 1, D), lambda b, i: (0, 0, 0)),
            pl.BlockSpec((1, G_ROWS, 1), lambda b, i: (0, 0, 0)),
            pl.BlockSpec((192, tm), lambda b, i: (0, i)),
            pl.BlockSpec((1, IN_COLS, D), lambda b, i: (0, 0, 0)),
            pl.BlockSpec((1, MLA_HEADS * MLA_QK, MLA_Q_RANK), lambda b, i: (0, 0, 0)),
            pl.BlockSpec((1, MLA_HEADS * (MLA_NOPE + MLA_V), MLA_KV_RANK), lambda b, i: (0, 0, 0)),
        ],
        out_specs=[
            pl.BlockSpec((1, N_MAPS, LANES, tm), lambda b, i: (b, 0, 0, i)),
            pl.BlockSpec((1, N_KSLAB, tm, LANES), lambda b, i: (b, 0, i, 0)),
            pl.BlockSpec((1, N_VHEAD, HEAD_VP, tm), lambda b, i: (b, 0, 0, i)),
        ],
        out_shape=[
            jax.ShapeDtypeStruct((B, N_MAPS, LANES, N), BF16),
            jax.ShapeDtypeStruct((B, N_KSLAB, N, LANES), BF16),
            jax.ShapeDtypeStruct((B, N_VHEAD, HEAD_VP, N), BF16),
        ],
        compiler_params=_cparams(("parallel", "parallel"), V7X_VMEM_LIMIT),
        name="qkv_proj",
    )(xc, mod, g1, gcol, rope_t, w_in_t, w_uq_t, w_ukv_t)


def _attn_kernel(lam_ref, gout_ref, q_ref, k_ref, v_ref, *rest, lam_init, aliased):
    if aliased:
        _, o_ref, m_sc, acc_sc = rest
    else:
        o_ref, m_sc, acc_sc = rest
    ki = pl.program_id(2)

    @pl.when(ki == 0)
    def _():
        m_sc[...] = jnp.full(m_sc.shape, NEG_BIG, F32)
        acc_sc[...] = jnp.zeros(acc_sc.shape, F32)

    def scores(mp):
        return jnp.dot(k_ref[0, _map_kslab(mp)], q_ref[0, mp], preferred_element_type=F32)

    pending = [scores(mp) for mp in range(QK_LOOKAHEAD)]
    for mp in range(N_MAPS):
        s = pending.pop(0)
        if mp + QK_LOOKAHEAD < N_MAPS:
            pending.append(scores(mp + QK_LOOKAHEAD))
        m_old = m_sc[mp]
        m_new = jnp.maximum(m_old, jnp.max(s, axis=0, keepdims=True))
        alpha = jnp.exp2(m_old - m_new)
        p = jnp.exp2(s - m_new).astype(BF16)
        acc_sc[mp] = alpha * acc_sc[mp] + jnp.dot(v_ref[0, _map_vhead(mp)], p, preferred_element_type=F32)
        m_sc[mp] = m_new

    @pl.when(ki == pl.num_programs(2) - 1)
    def _():
        lv = lam_ref[...]
        lam = (jnp.exp(jnp.sum(lv[0:1] * lv[1:2], axis=1, keepdims=True))
               - jnp.exp(jnp.sum(lv[2:3] * lv[3:4], axis=1, keepdims=True)) + lam_init)

        def normalised(mp):
            a = acc_sc[mp]
            return a[0:HEAD_V] / a[HEAD_V:HEAD_V + 1]

        heads = [normalised(hh) for hh in range(MLA_HEADS)]
        for hh in range(DIFF_HEADS):
            od = normalised(4 + 2 * hh) - lam * normalised(5 + 2 * hh)
            ms = jnp.mean(od * od, axis=0, keepdims=True)
            heads.append(od * lax.rsqrt(ms + EPS) * gout_ref[...] * (1.0 - lam_init))
        heads += [normalised(12 + hh) for hh in range(GQA_HEADS)]
        for j in range(len(heads) // 2):
            pair = jnp.concatenate([heads[2 * j], heads[2 * j + 1]], axis=0)
            o_ref[0, :, j * LANES:(j + 1) * LANES] = pair.T.astype(BF16)


def _attn_call(lamv, gout, q, k, v, att_in, *, lam_init, tq, tk, q_off, n_q, k_off, n_k, n_tok):
    B = q.shape[0]
    qo, ko = q_off // tq, k_off // tk
    in_specs = [
        pl.BlockSpec((4, DIFF_DK), lambda b, i, j: (0, 0)),
        pl.BlockSpec((HEAD_V, 1), lambda b, i, j: (0, 0)),
        pl.BlockSpec((1, N_MAPS, LANES, tq), lambda b, i, j: (b, 0, 0, qo + i)),
        pl.BlockSpec((1, N_KSLAB, tk, LANES), lambda b, i, j: (b, 0, ko + j, 0)),
        pl.BlockSpec((1, N_VHEAD, HEAD_VP, tk), lambda b, i, j: (b, 0, 0, ko + j)),
    ]
    args = [lamv, gout, q, k, v]
    aliases = {}
    if att_in is not None:
        in_specs.append(pl.BlockSpec(memory_space=pl.ANY))
        args.append(att_in)
        aliases = {5: 0}
    return pl.pallas_call(
        functools.partial(_attn_kernel, lam_init=lam_init, aliased=att_in is not None),
        grid=(B, n_q // tq, n_k // tk),
        in_specs=in_specs,
        out_specs=pl.BlockSpec((1, tq, 16 * HEAD_V), lambda b, i, j: (b, qo + i, 0)),
        out_shape=jax.ShapeDtypeStruct((B, n_tok, 16 * HEAD_V), BF16),
        scratch_shapes=[pltpu.VMEM((N_MAPS, 1, tq), F32), pltpu.VMEM((N_MAPS, HEAD_VP, tq), F32)],
        input_output_aliases=aliases,
        compiler_params=_cparams(("parallel", "parallel", "arbitrary"), V7X_VMEM_LIMIT),
        name="attention",
    )(*args)


def _post_kernel(att_ref, x_ref, mod_ref, g2_ref, wo_ref, wr_ref, x1_ref, h2_ref, aff_ref):
    a = jnp.dot(att_ref[0], wo_ref[0], preferred_element_type=F32)
    x1 = x_ref[0] + mod_ref[0, 0, 0:1, :] * a
    x1_ref[0] = x1
    ms = jnp.mean(x1 * x1, axis=-1, keepdims=True)
    h2 = (x1 * lax.rsqrt(ms + EPS) * g2_ref[0]) * (1.0 + mod_ref[0, 0, 2:3, :]) + mod_ref[0, 0, 1:2, :]
    hb = h2.astype(BF16)
    h2_ref[0] = hb
    lg = lax.dot_general(wr_ref[0], hb, (((1,), (1,)), ((), ())), preferred_element_type=F32)
    ex = jnp.exp(lg - jnp.max(lg, axis=0, keepdims=True))
    aff_ref[0] = ex / jnp.sum(ex, axis=0, keepdims=True)


def _post_call(att, xc, mod, g2, w_o, w_r_t, *, tm, n_lat_tiles, n_out):
    B, _, D = xc.shape
    nt = n_out // tm
    E = w_r_t.shape[1]
    stream = lambda i: jnp.minimum(i // n_lat_tiles, 1)
    return pl.pallas_call(
        _post_kernel,
        grid=(B, nt),
        in_specs=[
            pl.BlockSpec((1, tm, D), lambda b, i: (b, i, 0)),
            pl.BlockSpec((1, tm, D), lambda b, i: (b, i, 0)),
            pl.BlockSpec((1, 1, 3, D), lambda b, i: (b, stream(i), 0, 0)),
            pl.BlockSpec((1, 1, D), lambda b, i: (0, 0, 0)),
            pl.BlockSpec((1, D, D), lambda b, i: (0, 0, 0)),
            pl.BlockSpec((1, E, D), lambda b, i: (0, 0, 0)),
        ],
        out_specs=[
            pl.BlockSpec((1, tm, D), lambda b, i: (b, i, 0)),
            pl.BlockSpec((1, tm, D), lambda b, i: (b, i, 0)),
            pl.BlockSpec((1, E, tm), lambda b, i: (b, 0, i)),
        ],
        out_shape=[
            jax.ShapeDtypeStruct((B, n_out, D), F32),
            jax.ShapeDtypeStruct((B, n_out, D), BF16),
            jax.ShapeDtypeStruct((B, E, n_out), F32),
        ],
        compiler_params=_cparams(("parallel", "parallel"), 40 * 1024 * 1024),
        name="outproj_router",
    )(att, xc, mod, g2, w_o, w_r_t)


def _route_kernel(aff_ref, pos_ref, post_ref, p_ref, *, cap, tt):
    E, n = aff_ref.shape[1], aff_ref.shape[2]
    bits = lax.bitcast_convert_type(aff_ref[0], I32)

    def count_ge(th):
        return jnp.sum(jnp.where(bits >= th, 1.0, 0.0), axis=1, keepdims=True)

    def search(_, lohi):
        lo, hi = lohi
        mid = lo + lax.shift_right_logical(hi - lo + 1, 1)
        ok = count_ge(mid) >= cap
        return jnp.where(ok, mid, lo), jnp.where(ok, hi, mid - 1)

    lo0 = jnp.zeros((E, 1), I32)
    hi0 = jnp.full((E, 1), 0x7F800000, I32)
    tau, _ = lax.fori_loop(0, 32, search, (lo0, hi0))
    n_gt = jnp.sum(jnp.where(bits > tau, 1.0, 0.0), axis=1, keepdims=True)
    ties_kept = cap - n_gt

    ri = lax.broadcasted_iota(I32, (LANES, LANES), 0)
    ci = lax.broadcasted_iota(I32, (LANES, LANES), 1)
    strict_upper = jnp.where(ri < ci, 1.0, 0.0).astype(BF16)
    lane = lax.broadcasted_iota(I32, (E, LANES), 1)
    filler = jnp.full((LANES - E, LANES), -1.0, F32)

    c_eq = jnp.zeros((E, 1), F32)
    c_sel = jnp.zeros((E, 1), F32)
    starts = jnp.zeros((E, LANES), F32)
    for j in range(n // LANES):
        if (j * LANES) % tt == 0:
            starts = jnp.where(lane == (j * LANES) // tt, c_sel, starts)
        blk = bits[:, j * LANES:(j + 1) * LANES]
        gt = blk > tau
        eq = blk == tau
        eq_f = jnp.where(eq, 1.0, 0.0)
        rank = jnp.dot(eq_f.astype(BF16), strict_upper, preferred_element_type=F32) + c_eq
        sel = gt | (eq & (rank < ties_kept))
        sel_f = jnp.where(sel, 1.0, 0.0)
        pos = jnp.dot(sel_f.astype(BF16), strict_upper, preferred_element_type=F32) + c_sel
        pos = jnp.where(sel, pos, -1.0)
        pos_ref[0, :, j * LANES:(j + 1) * LANES] = pos.astype(I32)
        post_ref[0, j * LANES:(j + 1) * LANES, :] = (
            jnp.concatenate([pos, filler], axis=0).T[:, 0:E].astype(I32))
        c_eq = c_eq + jnp.sum(eq_f, axis=1, keepdims=True)
        c_sel = c_sel + jnp.sum(sel_f, axis=1, keepdims=True)
    p_ref[0] = starts.astype(I32)


def _route_call(aff_t, *, n, col_block, cap, tt):
    B, E, _ = aff_t.shape
    return pl.pallas_call(
        functools.partial(_route_kernel, cap=cap, tt=tt),
        grid=(B,),
        in_specs=[pl.BlockSpec((1, E, n), lambda b: (b, 0, col_block))],
        out_specs=[pl.BlockSpec((1, E, n), lambda b: (b, 0, 0)),
                   pl.BlockSpec((1, n, E), lambda b: (b, 0, 0)),
                   pl.BlockSpec((1, E, LANES), lambda b: (b, 0, 0))],
        out_shape=[jax.ShapeDtypeStruct((B, E, n), I32),
                   jax.ShapeDtypeStruct((B, n, E), I32),
                   jax.ShapeDtypeStruct((B, E, LANES), I32)],
        compiler_params=_cparams(("parallel",), 40 * 1024 * 1024),
        name="route",
    )(aff_t)


def _dispatch_kernel(p_ref, pos_ref, aff_ref, h_ref, xe_ref, gs_ref, acc, gacc, *, cap, tt, sb, n_exp):
    b, e = pl.program_id(0), pl.program_id(1)
    n = pos_ref.shape[2]
    win = 2 * sb
    acc[...] = jnp.zeros(acc.shape, F32)
    gacc[...] = jnp.zeros(gacc.shape, F32)
    base = (b * n_exp + e) * LANES

    def tile(i, carry):
        row0 = pl.multiple_of((p_ref[base + i] // sb) * sb, sb)
        t0 = pl.multiple_of(i * tt, tt)
        rel = pos_ref[0, :, pl.ds(t0, tt)] - row0
        hit = lax.broadcasted_iota(I32, (win, tt), 0) == rel
        rows = jnp.dot(jnp.where(hit, 1.0, 0.0).astype(BF16), h_ref[0, pl.ds(t0, tt), :],
                       preferred_element_type=F32)
        acc[pl.ds(row0, win), :] += rows
        gacc[pl.ds(row0, win), :] += jnp.sum(jnp.where(hit, aff_ref[0, :, pl.ds(t0, tt)], 0.0),
                                             axis=1, keepdims=True)
        return carry

    lax.fori_loop(0, n // tt, tile, 0)
    xe_ref[0, 0] = acc[0:cap, :].astype(BF16)
    gs_ref[0, 0] = gacc[0:cap, :]


def _dispatch_call(starts, pos, aff, h2, *, n, row_block, cap, tt, sb):
    B, E, _ = pos.shape
    D = h2.shape[2]
    return pl.pallas_call(
        functools.partial(_dispatch_kernel, cap=cap, tt=tt, sb=sb, n_exp=E),
        grid_spec=pltpu.PrefetchScalarGridSpec(
            num_scalar_prefetch=1,
            grid=(B, E),
            in_specs=[
                pl.BlockSpec((1, 1, n), lambda b, e, p: (b * E + e, 0, 0)),
                pl.BlockSpec((1, 1, n), lambda b, e, p: (b * E + e, 0, 0)),
                pl.BlockSpec((1, n, D), lambda b, e, p: (b, row_block, 0), pipeline_mode=pl.Buffered(1)),
            ],
            out_specs=[pl.BlockSpec((1, 1, cap, D), lambda b, e, p: (e, b, 0, 0)),
                       pl.BlockSpec((1, 1, cap, 1), lambda b, e, p: (e, b, 0, 0))],
            scratch_shapes=[pltpu.VMEM((cap + sb, D), F32), pltpu.VMEM((cap + sb, 1), F32)],
        ),
        out_shape=[jax.ShapeDtypeStruct((E, B, cap, D), BF16),
                   jax.ShapeDtypeStruct((E, B, cap, 1), F32)],
        compiler_params=_cparams(("parallel", "arbitrary"), V7X_VMEM_LIMIT),
        name="moe_dispatch",
    )(starts.reshape(-1), pos.reshape(B * E, 1, n), aff.reshape(B * E, 1, n), h2)


def _ffn_kernel(x_ref, gs_ref, wg_ref, wu_ref, wd_ref, y_ref, *, fc):
    nb, cap, D = x_ref.shape[1], x_ref.shape[2], x_ref.shape[3]
    FF = wg_ref.shape[2]
    x = x_ref[0].reshape(nb * cap, D)
    y = jnp.zeros((nb * cap, D), F32)
    for c in range(FF // fc):
        g = jnp.dot(x, wg_ref[0, :, c * fc:(c + 1) * fc], preferred_element_type=F32)
        u = jnp.dot(x, wu_ref[0, :, c * fc:(c + 1) * fc], preferred_element_type=F32)
        hid = (g / (1.0 + jnp.exp(-g))) * u
        y = y + jnp.dot(hid.astype(BF16), wd_ref[0, c * fc:(c + 1) * fc, :], preferred_element_type=F32)
    y = y * gs_ref[0].reshape(nb * cap, 1)
    y_ref[0] = y.reshape(nb, cap, D).astype(BF16)


def _ffn_call(xe, gs, wg, wu, wd, *, nb):
    E, B, cap, D = xe.shape
    FF = wg.shape[2]
    return pl.pallas_call(
        functools.partial(_ffn_kernel, fc=512),
        grid=(E, B // nb),
        in_specs=[
            pl.BlockSpec((1, nb, cap, D), lambda e, b: (e, b, 0, 0)),
            pl.BlockSpec((1, nb, cap, 1), lambda e, b: (e, b, 0, 0)),
            pl.BlockSpec((1, D, FF), lambda e, b: (e, 0, 0)),
            pl.BlockSpec((1, D, FF), lambda e, b: (e, 0, 0)),
            pl.BlockSpec((1, FF, D), lambda e, b: (e, 0, 0)),
        ],
        out_specs=pl.BlockSpec((1, nb, cap, D), lambda e, b: (e, b, 0, 0)),
        out_shape=jax.ShapeDtypeStruct((E, B, cap, D), BF16),
        compiler_params=_cparams(("parallel", "arbitrary"), V7X_VMEM_LIMIT),
        name="moe_ffn",
    )(xe, gs, wg, wu, wd)


def _combine_kernel(p_ref, x1_ref, post_ref, ga_ref, *rest, sb, eg, n_exp, aliased):
    ye_refs = rest[:2 * eg]
    if aliased:
        _, o_ref, acc = rest[2 * eg:]
    else:
        o_ref, acc = rest[2 * eg:]
    b, i, g = pl.program_id(0), pl.program_id(1), pl.program_id(2)
    tt = x1_ref.shape[1]

    @pl.when(g == 0)
    def _():
        acc[...] = jnp.zeros(acc.shape, F32)

    post = post_ref[0]
    lane = lax.broadcasted_iota(I32, post.shape, 1)
    slot = lax.broadcasted_iota(I32, (tt, sb), 1)
    for r in range(eg):
        e = g * eg + r
        col = jnp.sum(jnp.where(lane == e, post, 0), axis=1, keepdims=True)
        j0 = p_ref[(b * n_exp + e) * LANES + i] // sb
        for k in range(2):
            hit = col == (slot + (j0 + k) * sb)
            acc[...] += jnp.dot(jnp.where(hit, 1.0, 0.0).astype(BF16), ye_refs[2 * r + k][0, 0],
                                preferred_element_type=F32)

    @pl.when(g == pl.num_programs(2) - 1)
    def _():
        o_ref[0] = x1_ref[0] + ga_ref[0, 0] * acc[...]


def _combine_call(starts, x1, post, ga, ye, x_in, *, n, row_block, cap, tt, sb, eg, n_tok):
    E, B, _, D = ye.shape
    nblk = cap // sb
    rb = row_block * (n // tt)

    def ye_spec(r, k):
        def imap(b, i, g, p):
            e = g * eg + r
            return (e, b, jnp.minimum(p[(b * E + e) * LANES + i] // sb + k, nblk - 1), 0)
        return pl.BlockSpec((1, 1, sb, D), imap)

    in_specs = [
        pl.BlockSpec((1, tt, D), lambda b, i, g, p: (b, rb + i, 0)),
        pl.BlockSpec((1, tt, E), lambda b, i, g, p: (b, i, 0)),
        pl.BlockSpec((1, 1, 1, D), lambda b, i, g, p: (b, min(row_block, 1), 0, 0)),
    ] + [ye_spec(r, k) for r in range(eg) for k in range(2)]
    args = [x1, post, ga] + [ye] * (2 * eg)
    aliases = {}
    if x_in is not None:
        in_specs.append(pl.BlockSpec(memory_space=pl.ANY))
        args.append(x_in)
        aliases = {len(args): 0}
    return pl.pallas_call(
        functools.partial(_combine_kernel, sb=sb, eg=eg, n_exp=E, aliased=x_in is not None),
        grid_spec=pltpu.PrefetchScalarGridSpec(
            num_scalar_prefetch=1,
            grid=(B, n // tt, E // eg),
            in_specs=in_specs,
            out_specs=pl.BlockSpec((1, tt, D), lambda b, i, g, p: (b, rb + i, 0)),
            scratch_shapes=[pltpu.VMEM((tt, D), F32)],
        ),
        out_shape=jax.ShapeDtypeStruct((B, n_tok, D), F32),
        input_output_aliases=aliases,
        compiler_params=_cparams(("parallel", "parallel", "arbitrary"), 40 * 1024 * 1024),
        name="moe_combine",
    )(starts.reshape(-1), *args)


def _rope_tables_t(S, n_ctx):
    rows = S // GRID_W
    r = jnp.broadcast_to(jnp.arange(rows, dtype=F32)[:, None], (rows, GRID_W)).reshape(-1)
    col = jnp.broadcast_to(jnp.arange(GRID_W, dtype=F32)[None, :], (rows, GRID_W)).reshape(-1)
    out = []
    for rot in (MLA_ROPE, GQA_DH):
        axis_dim = rot // 2
        inv = ROPE_THETA ** (-jnp.arange(0, axis_dim, 2, dtype=F32) / axis_dim)
        ang = jnp.concatenate([r[:, None] * inv, col[:, None] * inv], axis=-1)
        cos = jnp.repeat(jnp.cos(ang), 2, axis=1)
        sin = jnp.repeat(jnp.sin(ang), 2, axis=1) * jnp.tile(jnp.array([-1.0, 1.0], F32), rot // 2)
        cos = jnp.concatenate([cos, jnp.ones((n_ctx, rot), F32)], axis=0)
        sin = jnp.concatenate([sin, jnp.zeros((n_ctx, rot), F32)], axis=0)
        out += [cos.T, sin.T]
    return jnp.concatenate(out, axis=0)


def _pick(n, candidates):
    for c in candidates:
        if n % c == 0:
            return c
    raise ValueError(f"no tile size for {n}")


def _moe_stream(aff_t, h2, x1, ga, wg, wu, wd, x_in, *, n, row_block, n_tok, nb):
    B, E, _ = aff_t.shape
    cap = CAPACITY_FACTOR * n // E
    tt = _pick(n, (256, 128))
    sb = min(tt, cap)
    pos, post, starts = _route_call(aff_t, n=n, col_block=row_block, cap=cap, tt=tt)
    aff_s = lax.slice_in_dim(aff_t, row_block * n, (row_block + 1) * n, axis=2)
    xe, gs = _dispatch_call(starts, pos, aff_s, h2, n=n, row_block=row_block, cap=cap, tt=tt, sb=sb)
    ye = _ffn_call(xe, gs, wg, wu, wd, nb=nb)
    return _combine_call(starts, x1, post, ga, ye, x_in, n=n, row_block=row_block, cap=cap, tt=tt, sb=sb,
                         eg=4, n_tok=n_tok)


def kernel(x, c, ctx, c_ctx, w_ada, b_ada, g_norm1, w_in, g_cq, g_ckv, w_uq, w_ukv, g_mla_q, g_mla_k,
           g_diff_q, g_diff_k, lam_q1, lam_k1, lam_q2, lam_k2, g_diff_out, g_gqa_q, g_gqa_k, w_o, g_norm2,
           w_router, w_gate, w_up, w_down):
    B, S, D = x.shape
    n_ctx = ctx.shape[1]
    N = S + n_ctx
    L = w_ada.shape[0]
    assert S % n_ctx == 0 and S % GRID_W == 0
    tm = _pick(n_ctx, (256, 128))
    n_lat_tiles = S // tm
    tq = _pick(S, (512, 256, 128))
    tk = _pick(N, (1408, 768, 512, 384, 256, 128))

    R = -(-(B + 1) // 8) * 8
    cc = jnp.zeros((R, D), F32).at[:B].set(c).at[B].set(c_ctx)
    mods = _ada_call(cc, w_ada, b_ada).reshape(L, R, 6, D)

    rope_t = _rope_tables_t(S, n_ctx)
    w_in_t = jnp.swapaxes(w_in, 1, 2).astype(BF16)
    w_uq_t = jnp.swapaxes(w_uq, 1, 2).astype(BF16)
    w_ukv_t = jnp.swapaxes(w_ukv, 1, 2).astype(BF16)
    w_o_b = w_o.astype(BF16)
    w_r_t = jnp.swapaxes(w_router, 1, 2).astype(BF16)
    wg_b, wu_b, wd_b = w_gate.astype(BF16), w_up.astype(BF16), w_down.astype(BF16)
    gcol = jnp.concatenate([g_cq, g_ckv, g_mla_q, g_mla_k, g_diff_q, g_diff_k, g_gqa_q, g_gqa_k],
                           axis=1)[:, :, None]
    lamv = jnp.stack([lam_q1, lam_k1, lam_q2, lam_k2], axis=1)

    xc = jnp.concatenate([x, ctx], axis=1)
    for l in range(L):
        last = l == L - 1
        lam_init = 0.8 - 0.6 * math.exp(-0.3 * l)
        m = mods[l]

        def per_stream(idx):
            lat = m[:B][:, idx, :]
            cx = jnp.broadcast_to(m[B][idx, :][None], lat.shape)
            return jnp.stack([lat, cx], axis=1)

        q, k, v = _proj_call(xc, per_stream(jnp.array([0, 1])), g_norm1[l][None, None], gcol[l][None],
                             rope_t, w_in_t[l][None], w_uq_t[l][None], w_ukv_t[l][None],
                             tm=tm, n_lat_tiles=n_lat_tiles)
        gout = g_diff_out[l][:, None]
        att = _attn_call(lamv[l], gout, q, k, v, None, lam_init=lam_init, tq=tq, tk=tk,
                         q_off=0, n_q=S, k_off=0, n_k=N, n_tok=N)
        n_out = S if last else N
        if not last:
            att = _attn_call(lamv[l], gout, q, k, v, att, lam_init=lam_init, tq=n_ctx, tk=n_ctx,
                             q_off=S, n_q=n_ctx, k_off=S, n_k=n_ctx, n_tok=N)
        x1, h2, aff_t = _post_call(att, xc, per_stream(jnp.array([2, 3, 4])), g_norm2[l][None, None],
                                   w_o_b[l][None], w_r_t[l][None], tm=tm, n_lat_tiles=n_lat_tiles, n_out=n_out)
        ga2 = per_stream(jnp.array([5]))
        xc = _moe_stream(aff_t, h2, x1, ga2, wg_b[l], wu_b[l], wd_b[l], None,
                         n=S, row_block=0, n_tok=n_out, nb=1)
        if not last:
            xc = _moe_stream(aff_t, h2, x1, ga2, wg_b[l], wu_b[l], wd_b[l], xc,
                             n=n_ctx, row_block=S // n_ctx, n_tok=n_out, nb=B)
    return xc
```

```python
import functools
import math

import jax
import jax.numpy as jnp
from jax import lax
from jax.experimental import pallas as pl
from jax.experimental.pallas import tpu as pltpu

F32 = jnp.float32
BF16 = jnp.bfloat16
I32 = jnp.int32

GRID_W = 64
ROPE_THETA = 10000.0
EPS = 1e-6

MLA_HEADS, MLA_Q_RANK, MLA_KV_RANK, MLA_NOPE, MLA_ROPE, MLA_V = 4, 256, 128, 64, 32, 64
MLA_QK = MLA_NOPE + MLA_ROPE
DIFF_HEADS, DIFF_DK = 4, 32
DIFF_DV = 2 * DIFF_DK
GQA_HEADS, GQA_KV_HEADS, GQA_DH = 8, 2, 64
HEAD_V = 64
HEAD_VP = HEAD_V + 8
QK_LOOKAHEAD = 2
SPEC_MAX_GROWTH = 100.0
N_EXPERTS = 16
CAPACITY_FACTOR = 2

O_CQ, O_CKV, O_KR, O_DQ, O_DK, O_DV, O_GQ, O_GK, O_GV, IN_COLS = (
    0, 256, 384, 416, 672, 928, 1184, 1696, 1824, 1952)
G_CQ, G_CKV, G_MQ, G_MK, G_DQ, G_DKK, G_GQ, G_GK, G_ROWS = 0, 256, 384, 480, 576, 608, 640, 704, 768

N_MAPS = 20
N_KSLAB = 7
N_VHEAD = 10
LANES = 128
LOG2E = 1.4426950408889634
NEG_BIG = -1e30

V7X_VMEM_LIMIT = 56 * 1024 * 1024


def _map_kslab(mp):
    if mp < 4:
        return mp
    if mp < 12:
        return 4 + (mp - 4) // 4
    return 6


def _map_vhead(mp):
    if mp < 4:
        return mp
    if mp < 12:
        return 4 + (mp - 4) // 2
    return 8 + (mp - 12) // 4


def _cparams(sem, vmem=None):
    return pltpu.CompilerParams(dimension_semantics=sem, vmem_limit_bytes=vmem)


def _ada_kernel(c_ref, w_ref, b_ref, o_ref):
    a = c_ref[...]
    s = a / (1.0 + jnp.exp(-a))
    o_ref[0] = jnp.dot(s.astype(BF16), w_ref[0].astype(BF16),
                       preferred_element_type=F32) + b_ref[0]


def _ada_call(cc, w_ada, b_ada):
    L, D, D6 = w_ada.shape
    R = cc.shape[0]
    tn = 1536
    return pl.pallas_call(
        _ada_kernel,
        grid=(L, D6 // tn),
        in_specs=[pl.BlockSpec((R, D), lambda l, j: (0, 0)),
                  pl.BlockSpec((1, D, tn), lambda l, j: (l, 0, j)),
                  pl.BlockSpec((1, 1, tn), lambda l, j: (l, 0, j))],
        out_specs=pl.BlockSpec((1, R, tn), lambda l, j: (l, 0, j)),
        out_shape=jax.ShapeDtypeStruct((L, R, D6), F32),
        compiler_params=_cparams(("parallel", "parallel"), 40 * 1024 * 1024),
        name="adaln",
    )(cc, w_ada, b_ada.reshape(L, 1, D6))


def _rms_rows(x3, g):
    ms = jnp.mean(x3 * x3, axis=1, keepdims=True)
    return x3 * lax.rsqrt(ms + EPS) * g[None]


def _rope_rows(x3, cos, sin_s):
    G, d, T = x3.shape
    x2 = x3.reshape(G * d, T)
    up = pltpu.roll(x2, G * d - 1, 0)
    dn = pltpu.roll(x2, 1, 0)
    even = (lax.broadcasted_iota(I32, (G * d, T), 0) % 2) == 0
    sw = jnp.where(even, up, dn).reshape(G, d, T)
    return x3 * cos[None] + sw * sin_s[None]


def _proj_kernel(x_ref, mod_ref, g1_ref, gcol_ref, rope_ref, win_ref, wuq_ref, wukv_ref,
                 q_ref, k_ref, v_ref):
    TM = x_ref.shape[1]
    x = x_ref[0]
    ms = jnp.mean(x * x, axis=-1, keepdims=True)
    h = (x * lax.rsqrt(ms + EPS) * g1_ref[0]) * (1.0 + mod_ref[0, 0, 1:2, :]) + mod_ref[0, 0, 0:1, :]
    hb = h.astype(BF16)
    zt = lax.dot_general(win_ref[0], hb, (((1,), (1,)), ((), ())),
                         preferred_element_type=F32)

    gcol = gcol_ref[0]
    cos32, sin32 = rope_ref[0:32, :], rope_ref[32:64, :]
    cos64, sin64 = rope_ref[64:128, :], rope_ref[128:192, :]
    zero32 = jnp.zeros((MLA_HEADS, 32, TM), F32)

    cqn = _rms_rows(zt[O_CQ:O_CKV][None], gcol[G_CQ:G_CKV])[0]
    qm = jnp.dot(wuq_ref[0], cqn.astype(BF16), preferred_element_type=F32)
    qm3 = _rms_rows(qm.reshape(MLA_HEADS, MLA_QK, TM), gcol[G_MQ:G_MK])
    qm_r = _rope_rows(qm3[:, MLA_NOPE:, :], cos32, sin32)
    sc = MLA_QK ** -0.5 * LOG2E
    q_ref[0, 0:4] = jnp.concatenate([qm3[:, :MLA_NOPE, :] * sc, qm_r * sc, zero32], axis=1).astype(BF16)

    ckvn = _rms_rows(zt[O_CKV:O_KR][None], gcol[G_CKV:G_MQ])[0]
    kv = jnp.dot(wukv_ref[0], ckvn.astype(BF16), preferred_element_type=F32)
    kv3 = kv.reshape(MLA_HEADS, MLA_NOPE + MLA_V, TM)
    nope = kv3[:, :MLA_NOPE, :]
    kr = zt[O_KR:O_DQ]
    ssq = jnp.sum(nope * nope, axis=1, keepdims=True) + jnp.sum(kr * kr, axis=0, keepdims=True)[None]
    inv = lax.rsqrt(ssq * (1.0 / MLA_QK) + EPS)
    gmk = gcol[G_MK:G_DQ]
    nope_n = nope * inv * gmk[None, :MLA_NOPE]
    kr_n = _rope_rows(kr[None] * inv * gmk[None, MLA_NOPE:], cos32, sin32)
    km = jnp.concatenate([nope_n, kr_n, zero32], axis=1)
    for hh in range(MLA_HEADS):
        k_ref[0, hh] = km[hh].T.astype(BF16)
    def with_ones(v3):
        tail = jnp.where(lax.broadcasted_iota(I32, (v3.shape[0], HEAD_VP - HEAD_V, TM), 1) == 0, 1.0, 0.0)
        return jnp.concatenate([v3, tail], axis=1).astype(BF16)

    v_ref[0, 0:4] = with_ones(kv3[:, MLA_NOPE:, :])

    qd = _rope_rows(_rms_rows(zt[O_DQ:O_DK].reshape(8, DIFF_DK, TM), gcol[G_DQ:G_DKK]), cos32, sin32)
    qd = qd * (DIFF_DK ** -0.5 * LOG2E)
    z32 = jnp.zeros((32, TM), F32)
    for gi in range(8):
        j = gi % 4
        parts = [z32] * j + [qd[gi]] + [z32] * (3 - j)
        q_ref[0, 4 + gi] = jnp.concatenate(parts, axis=0).astype(BF16)
    kd = _rope_rows(_rms_rows(zt[O_DK:O_DV].reshape(8, DIFF_DK, TM), gcol[G_DKK:G_GQ]), cos32, sin32)
    kd2 = kd.reshape(2, LANES, TM)
    for a in range(2):
        k_ref[0, 4 + a] = kd2[a].T.astype(BF16)
    v_ref[0, 4:8] = with_ones(zt[O_DV:O_GQ].reshape(DIFF_HEADS, DIFF_DV, TM))

    qg = _rope_rows(_rms_rows(zt[O_GQ:O_GK].reshape(GQA_HEADS, GQA_DH, TM), gcol[G_GQ:G_GK]), cos64, sin64)
    qg = qg * (GQA_DH ** -0.5 * LOG2E)
    z64 = jnp.zeros((64, TM), F32)
    for hh in range(GQA_HEADS):
        parts = [qg[hh], z64] if hh < 4 else [z64, qg[hh]]
        q_ref[0, 12 + hh] = jnp.concatenate(parts, axis=0).astype(BF16)
    kg = _rope_rows(_rms_rows(zt[O_GK:O_GV].reshape(GQA_KV_HEADS, GQA_DH, TM), gcol[G_GK:G_ROWS]), cos64, sin64)
    k_ref[0, 6] = kg.reshape(LANES, TM).T.astype(BF16)
    v_ref[0, 8:10] = with_ones(zt[O_GV:IN_COLS].reshape(GQA_KV_HEADS, GQA_DH, TM))


def _proj_call(xc, mod, g1, gcol, rope_t, w_in_t, w_uq_t, w_ukv_t, *, tm, n_lat_tiles):
    B, N, D = xc.shape
    nt = N // tm
    stream = lambda i: jnp.minimum(i // n_lat_tiles, 1)
    return pl.pallas_call(
        _proj_kernel,
        grid=(B, nt),
        in_specs=[
            pl.BlockSpec((1, tm, D), lambda b, i: (b, i, 0)),
            pl.BlockSpec((1, 1, 2, D), lambda b, i: (b, stream(i), 0, 0)),
            pl.BlockSpec((1, 1, D), lambda b, i: (0, 0, 0)),
            pl.BlockSpec((1, G_ROWS, 1), lambda b, i: (0, 0, 0)),
            pl.BlockSpec((192, tm), lambda b, i: (0, i)),
            pl.BlockSpec((1, IN_COLS, D), lambda b, i: (0, 0, 0)),
            pl.BlockSpec((1, MLA_HEADS * MLA_QK, MLA_Q_RANK), lambda b, i: (0, 0, 0)),
            pl.BlockSpec((1, MLA_HEADS * (MLA_NOPE + MLA_V), MLA_KV_RANK), lambda b, i: (0, 0, 0)),
        ],
        out_specs=[
            pl.BlockSpec((1, N_MAPS, LANES, tm), lambda b, i: (b, 0, 0, i)),
            pl.BlockSpec((1, N_KSLAB, tm, LANES), lambda b, i: (b, 0, i, 0)),
            pl.BlockSpec((1, N_VHEAD, HEAD_VP, tm), lambda b, i: (b, 0, 0, i)),
        ],
        out_shape=[
            jax.ShapeDtypeStruct((B, N_MAPS, LANES, N), BF16),
            jax.ShapeDtypeStruct((B, N_KSLAB, N, LANES), BF16),
            jax.ShapeDtypeStruct((B, N_VHEAD, HEAD_VP, N), BF16),
        ],
        compiler_params=_cparams(("parallel", "parallel"), V7X_VMEM_LIMIT),
        name="qkv_proj",
    )(xc, mod, g1, gcol, rope_t, w_in_t, w_uq_t, w_ukv_t)


def _attn_kernel(lam_ref, gout_ref, q_ref, k_ref, v_ref, *rest, aliased, speculate):
    rest = rest[1:] if aliased else rest
    if speculate:
        o_ref, m_sc, acc_sc, pv_sc, tm_sc, keep = rest
    else:
        o_ref, m_sc, acc_sc = rest
    ki = pl.program_id(2)

    def scores(mp):
        return jnp.dot(k_ref[0, _map_kslab(mp)], q_ref[0, mp], preferred_element_type=F32)

    def for_each_map(body):
        pending = [scores(mp) for mp in range(QK_LOOKAHEAD)]
        for mp in range(N_MAPS):
            s = pending.pop(0)
            if mp + QK_LOOKAHEAD < N_MAPS:
                pending.append(scores(mp + QK_LOOKAHEAD))
            body(mp, s)

    def exact_update(mp, s):
        m_old = m_sc[mp]
        m_new = jnp.maximum(m_old, jnp.max(s, axis=0, keepdims=True))
        p = jnp.exp2(s - m_new).astype(BF16)
        acc_sc[mp] = (jnp.exp2(m_old - m_new) * acc_sc[mp]
                      + jnp.dot(v_ref[0, _map_vhead(mp)], p, preferred_element_type=F32))
        m_sc[mp] = m_new

    @pl.when(ki == 0)
    def _():
        m_sc[...] = jnp.full(m_sc.shape, NEG_BIG, F32)
        acc_sc[...] = jnp.zeros(acc_sc.shape, F32)

    if not speculate:
        for_each_map(exact_update)
    else:
        keep[0] = 0

        @pl.when(ki > 0)
        def _():
            growth = []

            def one_pass(mp, s):
                r = m_sc[mp]
                p = jnp.exp2(s - r).astype(BF16)
                t_max = jnp.max(s, axis=0, keepdims=True)
                tm_sc[mp] = t_max
                pv_sc[mp] = jnp.dot(v_ref[0, _map_vhead(mp)], p, preferred_element_type=F32)
                growth.append(t_max - r)

            for_each_map(one_pass)
            worst = functools.reduce(jnp.maximum, growth)
            keep[0] = (jnp.max(worst) <= SPEC_MAX_GROWTH).astype(I32)

        kept = keep[0] == 1

        @pl.when(kept)
        def _():
            for mp in range(N_MAPS):
                r = m_sc[mp]
                m_new = jnp.maximum(r, tm_sc[mp])
                acc_sc[mp] = (acc_sc[mp] + pv_sc[mp]) * jnp.exp2(r - m_new)
                m_sc[mp] = m_new

        @pl.when(jnp.logical_not(kept))
        def _():
            for_each_map(exact_update)

    @pl.when(ki == pl.num_programs(2) - 1)
    def _():
        lv = lam_ref[...]
        lam_init = lv[4:5, 0:1]
        lam = (jnp.exp(jnp.sum(lv[0:1] * lv[1:2], axis=1, keepdims=True))
               - jnp.exp(jnp.sum(lv[2:3] * lv[3:4], axis=1, keepdims=True)) + lam_init)

        def normalised(mp):
            a = acc_sc[mp]
            return a[0:HEAD_V] / a[HEAD_V:HEAD_V + 1]

        heads = [normalised(hh) for hh in range(MLA_HEADS)]
        for hh in range(DIFF_HEADS):
            od = normalised(4 + 2 * hh) - lam * normalised(5 + 2 * hh)
            ms = jnp.mean(od * od, axis=0, keepdims=True)
            heads.append(od * lax.rsqrt(ms + EPS) * gout_ref[...] * (1.0 - lam_init))
        heads += [normalised(12 + hh) for hh in range(GQA_HEADS)]
        for j in range(len(heads) // 2):
            pair = jnp.concatenate([heads[2 * j], heads[2 * j + 1]], axis=0)
            o_ref[0, :, j * LANES:(j + 1) * LANES] = pair.T.astype(BF16)


def _attn_call(lamv, gout, q, k, v, att_in, *, tq, tk, q_off, n_q, k_off, n_k, n_tok):
    B = q.shape[0]
    qo, ko = q_off // tq, k_off // tk
    in_specs = [
        pl.BlockSpec((8, DIFF_DK), lambda b, i, j: (0, 0)),
        pl.BlockSpec((HEAD_V, 1), lambda b, i, j: (0, 0)),
        pl.BlockSpec((1, N_MAPS, LANES, tq), lambda b, i, j: (b, 0, 0, qo + i)),
        pl.BlockSpec((1, N_KSLAB, tk, LANES), lambda b, i, j: (b, 0, ko + j, 0)),
        pl.BlockSpec((1, N_VHEAD, HEAD_VP, tk), lambda b, i, j: (b, 0, 0, ko + j)),
    ]
    args = [lamv, gout, q, k, v]
    aliases = {}
    if att_in is not None:
        in_specs.append(pl.BlockSpec(memory_space=pl.ANY))
        args.append(att_in)
        aliases = {5: 0}
    speculate = n_k // tk > 1
    scratch = [pltpu.VMEM((N_MAPS, 1, tq), F32), pltpu.VMEM((N_MAPS, HEAD_VP, tq), F32)]
    if speculate:
        scratch += [pltpu.VMEM((N_MAPS, HEAD_VP, tq), F32), pltpu.VMEM((N_MAPS, 1, tq), F32),
                    pltpu.SMEM((1,), I32)]
    return pl.pallas_call(
        functools.partial(_attn_kernel, aliased=att_in is not None, speculate=speculate),
        grid=(B, n_q // tq, n_k // tk),
        in_specs=in_specs,
        out_specs=pl.BlockSpec((1, tq, 16 * HEAD_V), lambda b, i, j: (b, qo + i, 0)),
        out_shape=jax.ShapeDtypeStruct((B, n_tok, 16 * HEAD_V), BF16),
        scratch_shapes=scratch,
        input_output_aliases=aliases,
        compiler_params=_cparams(("parallel", "parallel", "arbitrary"), V7X_VMEM_LIMIT),
        name="attention",
    )(*args)


def _post_kernel(att_ref, x_ref, mod_ref, g2_ref, wo_ref, wr_ref, x1_ref, h2_ref, aff_ref):
    a = jnp.dot(att_ref[0], wo_ref[0], preferred_element_type=F32)
    x1 = x_ref[0] + mod_ref[0, 0, 0:1, :] * a
    x1_ref[0] = x1
    ms = jnp.mean(x1 * x1, axis=-1, keepdims=True)
    h2 = (x1 * lax.rsqrt(ms + EPS) * g2_ref[0]) * (1.0 + mod_ref[0, 0, 2:3, :]) + mod_ref[0, 0, 1:2, :]
    hb = h2.astype(BF16)
    h2_ref[0] = hb
    lg = lax.dot_general(wr_ref[0], hb, (((1,), (1,)), ((), ())), preferred_element_type=F32)
    ex = jnp.exp(lg - jnp.max(lg, axis=0, keepdims=True))
    aff_ref[0] = ex / jnp.sum(ex, axis=0, keepdims=True)


def _post_call(att, xc, mod, g2, w_o, w_r_t, *, tm, n_lat_tiles, n_out):
    B, _, D = xc.shape
    nt = n_out // tm
    E = w_r_t.shape[1]
    stream = lambda i: jnp.minimum(i // n_lat_tiles, 1)
    return pl.pallas_call(
        _post_kernel,
        grid=(B, nt),
        in_specs=[
            pl.BlockSpec((1, tm, D), lambda b, i: (b, i, 0)),
            pl.BlockSpec((1, tm, D), lambda b, i: (b, i, 0)),
            pl.BlockSpec((1, 1, 3, D), lambda b, i: (b, stream(i), 0, 0)),
            pl.BlockSpec((1, 1, D), lambda b, i: (0, 0, 0)),
            pl.BlockSpec((1, D, D), lambda b, i: (0, 0, 0)),
            pl.BlockSpec((1, E, D), lambda b, i: (0, 0, 0)),
        ],
        out_specs=[
            pl.BlockSpec((1, tm, D), lambda b, i: (b, i, 0)),
            pl.BlockSpec((1, tm, D), lambda b, i: (b, i, 0)),
            pl.BlockSpec((1, E, tm), lambda b, i: (b, 0, i)),
        ],
        out_shape=[
            jax.ShapeDtypeStruct((B, n_out, D), F32),
            jax.ShapeDtypeStruct((B, n_out, D), BF16),
            jax.ShapeDtypeStruct((B, E, n_out), F32),
        ],
        compiler_params=_cparams(("parallel", "parallel"), 40 * 1024 * 1024),
        name="outproj_router",
    )(att, xc, mod, g2, w_o, w_r_t)


def _route_kernel(aff_ref, pos_ref, post_ref, p_ref, *, cap):
    E, n = aff_ref.shape[1], aff_ref.shape[2]
    bits = lax.bitcast_convert_type(aff_ref[0], I32)

    def count_ge(th):
        return jnp.sum(jnp.where(bits >= th, 1.0, 0.0), axis=1, keepdims=True)

    def search(_, lohi):
        lo, hi = lohi
        mid = lo + lax.shift_right_logical(hi - lo + 1, 1)
        ok = count_ge(mid) >= cap
        return jnp.where(ok, mid, lo), jnp.where(ok, hi, mid - 1)

    lo0 = jnp.zeros((E, 1), I32)
    hi0 = jnp.full((E, 1), 0x7F800000, I32)
    tau, _ = lax.fori_loop(0, 32, search, (lo0, hi0))
    n_gt = jnp.sum(jnp.where(bits > tau, 1.0, 0.0), axis=1, keepdims=True)
    ties_kept = cap - n_gt

    ri = lax.broadcasted_iota(I32, (LANES, LANES), 0)
    ci = lax.broadcasted_iota(I32, (LANES, LANES), 1)
    strict_upper = jnp.where(ri < ci, 1.0, 0.0).astype(BF16)
    lane = lax.broadcasted_iota(I32, (E, LANES), 1)
    filler = jnp.full((LANES - E, LANES), -1.0, F32)

    c_eq = jnp.zeros((E, 1), F32)
    c_sel = jnp.zeros((E, 1), F32)
    starts = jnp.zeros((E, LANES), F32)
    for j in range(n // LANES):
        starts = jnp.where(lane == j, c_sel, starts)
        blk = bits[:, j * LANES:(j + 1) * LANES]
        gt = blk > tau
        eq = blk == tau
        eq_f = jnp.where(eq, 1.0, 0.0)
        rank = jnp.dot(eq_f.astype(BF16), strict_upper, preferred_element_type=F32) + c_eq
        sel = gt | (eq & (rank < ties_kept))
        sel_f = jnp.where(sel, 1.0, 0.0)
        pos = jnp.dot(sel_f.astype(BF16), strict_upper, preferred_element_type=F32) + c_sel
        pos = jnp.where(sel, pos, -1.0)
        pos_ref[0, :, j * LANES:(j + 1) * LANES] = pos.astype(I32)
        post_ref[0, j * LANES:(j + 1) * LANES, :] = (
            jnp.concatenate([pos, filler], axis=0).T[:, 0:E].astype(I32))
        c_eq = c_eq + jnp.sum(eq_f, axis=1, keepdims=True)
        c_sel = c_sel + jnp.sum(sel_f, axis=1, keepdims=True)
    p_ref[0] = starts.astype(I32)


def _route_call(aff_t, *, n, col_block, cap):
    B, E, _ = aff_t.shape
    return pl.pallas_call(
        functools.partial(_route_kernel, cap=cap),
        grid=(B,),
        in_specs=[pl.BlockSpec((1, E, n), lambda b: (b, 0, col_block))],
        out_specs=[pl.BlockSpec((1, E, n), lambda b: (b, 0, 0)),
                   pl.BlockSpec((1, n, E), lambda b: (b, 0, 0)),
                   pl.BlockSpec((1, E, LANES), lambda b: (b, 0, 0))],
        out_shape=[jax.ShapeDtypeStruct((B, E, n), I32),
                   jax.ShapeDtypeStruct((B, n, E), I32),
                   jax.ShapeDtypeStruct((B, E, LANES), I32)],
        compiler_params=_cparams(("parallel",), 40 * 1024 * 1024),
        name="route",
    )(aff_t)


def _dispatch_kernel(p_ref, pos_ref, aff_ref, h_ref, xe_ref, gs_ref, acc, gacc, *, cap, tt, sb, n_exp):
    b, e = pl.program_id(0), pl.program_id(1)
    n = pos_ref.shape[2]
    win = 2 * sb
    acc[...] = jnp.zeros(acc.shape, F32)
    gacc[...] = jnp.zeros(gacc.shape, F32)
    base = (b * n_exp + e) * LANES

    def tile(i, carry):
        row0 = pl.multiple_of((p_ref[base + i * (tt // LANES)] // sb) * sb, sb)
        t0 = pl.multiple_of(i * tt, tt)
        rel = pos_ref[0, :, pl.ds(t0, tt)] - row0
        hit = lax.broadcasted_iota(I32, (win, tt), 0) == rel
        rows = jnp.dot(jnp.where(hit, 1.0, 0.0).astype(BF16), h_ref[0, pl.ds(t0, tt), :],
                       preferred_element_type=F32)
        acc[pl.ds(row0, win), :] += rows
        gacc[pl.ds(row0, win), :] += jnp.sum(jnp.where(hit, aff_ref[0, :, pl.ds(t0, tt)], 0.0),
                                             axis=1, keepdims=True)
        return carry

    lax.fori_loop(0, n // tt, tile, 0)
    xe_ref[0, 0] = acc[0:cap, :].astype(BF16)
    gs_ref[0, 0] = gacc[0:cap, :]


def _dispatch_call(starts, pos, aff, h2, *, n, row_block, cap, tt, sb):
    B, E, _ = pos.shape
    D = h2.shape[2]
    return pl.pallas_call(
        functools.partial(_dispatch_kernel, cap=cap, tt=tt, sb=sb, n_exp=E),
        grid_spec=pltpu.PrefetchScalarGridSpec(
            num_scalar_prefetch=1,
            grid=(B, E),
            in_specs=[
                pl.BlockSpec((1, 1, n), lambda b, e, p: (b * E + e, 0, 0)),
                pl.BlockSpec((1, 1, n), lambda b, e, p: (b * E + e, 0, 0)),
                pl.BlockSpec((1, n, D), lambda b, e, p: (b, row_block, 0), pipeline_mode=pl.Buffered(1)),
            ],
            out_specs=[pl.BlockSpec((1, 1, cap, D), lambda b, e, p: (e, b, 0, 0)),
                       pl.BlockSpec((1, 1, cap, 1), lambda b, e, p: (e, b, 0, 0))],
            scratch_shapes=[pltpu.VMEM((cap + sb, D), F32), pltpu.VMEM((cap + sb, 1), F32)],
        ),
        out_shape=[jax.ShapeDtypeStruct((E, B, cap, D), BF16),
                   jax.ShapeDtypeStruct((E, B, cap, 1), F32)],
        compiler_params=_cparams(("parallel", "arbitrary"), V7X_VMEM_LIMIT),
        name="moe_dispatch",
    )(starts.reshape(-1), pos.reshape(B * E, 1, n), aff.reshape(B * E, 1, n), h2)


def _ffn_kernel(x_ref, gs_ref, wg_ref, wu_ref, wd_ref, y_ref, *, fc):
    nb, cap, D = x_ref.shape[1], x_ref.shape[2], x_ref.shape[3]
    FF = wg_ref.shape[2]
    x = x_ref[0].reshape(nb * cap, D)
    y = jnp.zeros((nb * cap, D), F32)
    for c in range(FF // fc):
        g = jnp.dot(x, wg_ref[0, :, c * fc:(c + 1) * fc], preferred_element_type=F32)
        u = jnp.dot(x, wu_ref[0, :, c * fc:(c + 1) * fc], preferred_element_type=F32)
        hid = (g / (1.0 + jnp.exp(-g))) * u
        y = y + jnp.dot(hid.astype(BF16), wd_ref[0, c * fc:(c + 1) * fc, :], preferred_element_type=F32)
    y = y * gs_ref[0].reshape(nb * cap, 1)
    y_ref[0] = y.reshape(nb, cap, D).astype(BF16)


def _ffn_call(xe, gs, wg, wu, wd, *, nb):
    E, B, cap, D = xe.shape
    FF = wg.shape[2]
    return pl.pallas_call(
        functools.partial(_ffn_kernel, fc=512),
        grid=(E, B // nb),
        in_specs=[
            pl.BlockSpec((1, nb, cap, D), lambda e, b: (e, b, 0, 0)),
            pl.BlockSpec((1, nb, cap, 1), lambda e, b: (e, b, 0, 0)),
            pl.BlockSpec((1, D, FF), lambda e, b: (e, 0, 0)),
            pl.BlockSpec((1, D, FF), lambda e, b: (e, 0, 0)),
            pl.BlockSpec((1, FF, D), lambda e, b: (e, 0, 0)),
        ],
        out_specs=pl.BlockSpec((1, nb, cap, D), lambda e, b: (e, b, 0, 0)),
        out_shape=jax.ShapeDtypeStruct((E, B, cap, D), BF16),
        compiler_params=_cparams(("parallel", "arbitrary"), V7X_VMEM_LIMIT),
        name="moe_ffn",
    )(xe, gs, wg, wu, wd)


def _combine_kernel(p_ref, x1_ref, post_ref, ga_ref, ye_ref, *rest, sb, win, aliased):
    o_ref = rest[1] if aliased else rest[0]
    b, i = pl.program_id(0), pl.program_id(1)
    n_exp, cap = ye_ref.shape[0], ye_ref.shape[2]
    tt = x1_ref.shape[1]
    post = post_ref[0]
    lane = lax.broadcasted_iota(I32, post.shape, 1)
    slot = lax.broadcasted_iota(I32, (tt, win), 1)
    acc = jnp.zeros(o_ref.shape[1:], F32)
    for e in range(n_exp):
        col = jnp.sum(jnp.where(lane == e, post, 0), axis=1, keepdims=True)
        row0 = jnp.minimum((p_ref[(b * n_exp + e) * LANES + i] // sb) * sb, cap - win)
        row0 = pl.multiple_of(row0, sb)
        hit = col == (slot + row0)
        acc = acc + jnp.dot(jnp.where(hit, 1.0, 0.0).astype(BF16), ye_ref[e, 0, pl.ds(row0, win), :],
                            preferred_element_type=F32)
    o_ref[0] = x1_ref[0] + ga_ref[0, 0] * acc


def _combine_call(starts, x1, post, ga, ye, x_in, *, n, row_block, cap, n_tok):
    E, B, _, D = ye.shape
    tt = LANES
    sb = min(tt, cap)
    win = min(2 * sb, cap)
    rb = row_block * (n // tt)
    in_specs = [
        pl.BlockSpec((1, tt, D), lambda b, i, p: (b, rb + i, 0)),
        pl.BlockSpec((1, tt, E), lambda b, i, p: (b, i, 0)),
        pl.BlockSpec((1, 1, 1, D), lambda b, i, p: (b, min(row_block, 1), 0, 0)),
        pl.BlockSpec((E, 1, cap, D), lambda b, i, p: (0, b, 0, 0), pipeline_mode=pl.Buffered(1)),
    ]
    args = [x1, post, ga, ye]
    aliases = {}
    if x_in is not None:
        in_specs.append(pl.BlockSpec(memory_space=pl.ANY))
        args.append(x_in)
        aliases = {len(args): 0}
    return pl.pallas_call(
        functools.partial(_combine_kernel, sb=sb, win=win, aliased=x_in is not None),
        grid_spec=pltpu.PrefetchScalarGridSpec(
            num_scalar_prefetch=1,
            grid=(B, n // tt),
            in_specs=in_specs,
            out_specs=pl.BlockSpec((1, tt, D), lambda b, i, p: (b, rb + i, 0)),
        ),
        out_shape=jax.ShapeDtypeStruct((B, n_tok, D), F32),
        input_output_aliases=aliases,
        compiler_params=_cparams(("parallel", "arbitrary"), V7X_VMEM_LIMIT),
        name="moe_combine",
    )(starts.reshape(-1), *args)


def _rope_tables_t(S, n_ctx):
    rows = S // GRID_W
    r = jnp.broadcast_to(jnp.arange(rows, dtype=F32)[:, None], (rows, GRID_W)).reshape(-1)
    col = jnp.broadcast_to(jnp.arange(GRID_W, dtype=F32)[None, :], (rows, GRID_W)).reshape(-1)
    out = []
    for rot in (MLA_ROPE, GQA_DH):
        axis_dim = rot // 2
        inv = ROPE_THETA ** (-jnp.arange(0, axis_dim, 2, dtype=F32) / axis_dim)
        ang = jnp.concatenate([r[:, None] * inv, col[:, None] * inv], axis=-1)
        cos = jnp.repeat(jnp.cos(ang), 2, axis=1)
        sin = jnp.repeat(jnp.sin(ang), 2, axis=1) * jnp.tile(jnp.array([-1.0, 1.0], F32), rot // 2)
        cos = jnp.concatenate([cos, jnp.ones((n_ctx, rot), F32)], axis=0)
        sin = jnp.concatenate([sin, jnp.zeros((n_ctx, rot), F32)], axis=0)
        out += [cos.T, sin.T]
    return jnp.concatenate(out, axis=0)


def _pick(n, candidates):
    for c in candidates:
        if n % c == 0:
            return c
    raise ValueError(f"no tile size for {n}")


def _moe_stream(aff_t, h2, x1, ga, wg, wu, wd, x_in, *, n, row_block, n_tok, nb):
    B, E, _ = aff_t.shape
    cap = CAPACITY_FACTOR * n // E
    tt = _pick(n, (256, 128))
    sb = min(tt, cap)
    pos, post, starts = _route_call(aff_t, n=n, col_block=row_block, cap=cap)
    aff_s = lax.slice_in_dim(aff_t, row_block * n, (row_block + 1) * n, axis=2)
    xe, gs = _dispatch_call(starts, pos, aff_s, h2, n=n, row_block=row_block, cap=cap, tt=tt, sb=sb)
    ye = _ffn_call(xe, gs, wg, wu, wd, nb=nb)
    return _combine_call(starts, x1, post, ga, ye, x_in, n=n, row_block=row_block, cap=cap, n_tok=n_tok)


def kernel(x, c, ctx, c_ctx, w_ada, b_ada, g_norm1, w_in, g_cq, g_ckv, w_uq, w_ukv, g_mla_q, g_mla_k,
           g_diff_q, g_diff_k, lam_q1, lam_k1, lam_q2, lam_k2, g_diff_out, g_gqa_q, g_gqa_k, w_o, g_norm2,
           w_router, w_gate, w_up, w_down):
    B, S, D = x.shape
    n_ctx = ctx.shape[1]
    N = S + n_ctx
    L = w_ada.shape[0]
    assert S % n_ctx == 0 and S % GRID_W == 0
    tm = _pick(n_ctx, (256, 128))
    n_lat_tiles = S // tm
    tq = _pick(S, (512, 256, 128))
    tk = _pick(N, (1408, 768, 512, 384, 256, 128))

    R = -(-(B + 1) // 8) * 8
    cc = jnp.zeros((R, D), F32).at[:B].set(c).at[B].set(c_ctx)
    mods = _ada_call(cc, w_ada, b_ada).reshape(L, R, 6, D)

    rope_t = _rope_tables_t(S, n_ctx)
    w_in_t = jnp.swapaxes(w_in, 1, 2).astype(BF16)
    w_uq_t = jnp.swapaxes(w_uq, 1, 2).astype(BF16)
    w_ukv_t = jnp.swapaxes(w_ukv, 1, 2).astype(BF16)
    w_o_b = w_o.astype(BF16)
    w_r_t = jnp.swapaxes(w_router, 1, 2).astype(BF16)
    wg_b, wu_b, wd_b = w_gate.astype(BF16), w_up.astype(BF16), w_down.astype(BF16)
    gcol = jnp.concatenate([g_cq, g_ckv, g_mla_q, g_mla_k, g_diff_q, g_diff_k, g_gqa_q, g_gqa_k],
                           axis=1)[:, :, None]
    lam_init = jnp.array([0.8 - 0.6 * math.exp(-0.3 * l) for l in range(L)], F32)
    lamv = jnp.stack([lam_q1, lam_k1, lam_q2, lam_k2, jnp.broadcast_to(lam_init[:, None], lam_q1.shape)]
                     + [jnp.zeros_like(lam_q1)] * 3, axis=1)

    xc = jnp.concatenate([x, ctx], axis=1)
    for l in range(L):
        last = l == L - 1
        m = mods[l]

        def per_stream(idx):
            lat = m[:B][:, idx, :]
            cx = jnp.broadcast_to(m[B][idx, :][None], lat.shape)
            return jnp.stack([lat, cx], axis=1)

        q, k, v = _proj_call(xc, per_stream(jnp.array([0, 1])), g_norm1[l][None, None], gcol[l][None],
                             rope_t, w_in_t[l][None], w_uq_t[l][None], w_ukv_t[l][None],
                             tm=tm, n_lat_tiles=n_lat_tiles)
        gout = g_diff_out[l][:, None]
        att = _attn_call(lamv[l], gout, q, k, v, None, tq=tq, tk=tk,
                         q_off=0, n_q=S, k_off=0, n_k=N, n_tok=N)
        n_out = S if last else N
        if not last:
            att = _attn_call(lamv[l], gout, q, k, v, att, tq=n_ctx, tk=n_ctx,
                             q_off=S, n_q=n_ctx, k_off=S, n_k=n_ctx, n_tok=N)
        x1, h2, aff_t = _post_call(att, xc, per_stream(jnp.array([2, 3, 4])), g_norm2[l][None, None],
                                   w_o_b[l][None], w_r_t[l][None], tm=tm, n_lat_tiles=n_lat_tiles, n_out=n_out)
        ga2 = per_stream(jnp.array([5]))
        xc = _moe_stream(aff_t, h2, x1, ga2, wg_b[l], wu_b[l], wd_b[l], None,
                         n=S, row_block=0, n_tok=n_out, nb=1)
        if not last:
            xc = _moe_stream(aff_t, h2, x1, ga2, wg_b[l], wu_b[l], wd_b[l], xc,
                             n=n_ctx, row_block=S // n_ctx, n_tok=n_out, nb=B)
    return xc
```
